```python
import jax, jax.numpy as jnp
from jax import lax
import numpy as np

D_MODEL = 1024
BATCH = 1
SEQ = 16384
DEPTH = 2

HEAD_DIM = 64
BLK = 128
RMS_EPS = 1e-6
NEG = -1e30

A_HEADS = 8
A_KV = 2
A_GROUP = A_HEADS // A_KV
A_WINDOW = 128
B_HEADS = 8
B_KV = 2
B_GROUP = B_HEADS // B_KV
B_WINDOW = 512
L_CMP = 32
D_CMP = 16
CMP_HID = 128
L_SEL = 64
N_SEL = 16
C_HEADS = 8
MIX_W = A_HEADS * HEAD_DIM
A_KVW = A_KV * HEAD_DIM
B_KVW = B_KV * HEAD_DIM
IN_WIDTHS = (MIX_W, A_KVW, A_KVW,
             MIX_W, B_KVW, B_KVW, B_KVW, B_KVW, B_KVW, B_KVW,
             B_HEADS * 3,
             MIX_W, MIX_W, MIX_W,
             3 * D_MODEL)
D_IN = sum(IN_WIDTHS)
PEER_HEADS = 8
N_KEYS = 128
N_EXPERTS = N_KEYS * N_KEYS
PK_DIM = 128
PEER_TOPK = 16

kernel_name = "hybrid_swa_nsa_stickbreak_peer"


def rms_norm(x, g):
    xf = x.astype(jnp.float32)
    y = xf * lax.rsqrt(jnp.mean(xf * xf, axis=-1, keepdims=True) + RMS_EPS)
    return (y * g.astype(jnp.float32)).astype(x.dtype)


def alibi_slopes(n):
    return jnp.exp2(-8.0 * jnp.arange(1, n + 1, dtype=jnp.float32) / n)


def split_cols(h, widths):
    out, start = [], 0
    for w in widths:
        out.append(h[..., start:start + w])
        start += w
    return out


def unblock(o):
    o = jnp.moveaxis(o, 0, 1)
    return o.reshape((o.shape[0], o.shape[1] * o.shape[2]) + o.shape[3:])


def banded_gqa(q, k, v, window, slopes, sinks=None):
    s_len = q.shape[1]
    dh = q.shape[-1]
    nb = s_len // BLK
    pad = ((0, 0), (window, 0), (0, 0), (0, 0))
    kp, vp = jnp.pad(k, pad), jnp.pad(v, pad)
    scale = dh ** -0.5

    def block(n):
        q0 = n * BLK
        qb = lax.dynamic_slice_in_dim(q, q0, BLK, axis=1)
        kb = lax.dynamic_slice_in_dim(kp, q0, window + BLK, axis=1)
        vb = lax.dynamic_slice_in_dim(vp, q0, window + BLK, axis=1)
        t = q0 + jnp.arange(BLK)
        s = q0 - window + jnp.arange(window + BLK)
        dist = t[:, None] - s[None, :]
        mask = (s[None, :] >= 0) & (dist >= 0) & (dist < window)
        logits = (jnp.einsum('bqhgd,bkhd->bhgqk', qb, kb).astype(jnp.float32) * scale
                  - slopes[None, :, :, None, None] * dist.astype(jnp.float32))
        logits = jnp.where(mask, logits, NEG)
        if sinks is not None:
            sink = jnp.broadcast_to(sinks.astype(jnp.float32)[None, :, :, None, None],
                                    logits.shape[:-1] + (1,))
            p = jax.nn.softmax(jnp.concatenate([logits, sink], axis=-1), axis=-1)[..., :-1]
        else:
            p = jax.nn.softmax(logits, axis=-1)
        return jnp.einsum('bhgqk,bkhd->bqhgd', p.astype(vb.dtype), vb)

    return unblock(lax.map(block, jnp.arange(nb)))


def nsa_compress(kv, pos, w1, w2):
    nc = (kv.shape[1] - L_CMP) // D_CMP + 1
    idx = jnp.arange(nc)[:, None] * D_CMP + jnp.arange(L_CMP)[None, :]
    blocks = kv[:, idx] + pos[None, None, :, None, :]
    hid = jax.nn.gelu(jnp.einsum('bclhd,ldf->bchf', blocks, w1))
    return jnp.einsum('bchf,fd->bchd', hid, w2)


def nsa_compressed_selected(q, kc, vc, ks, vs, slopes):
    b, s_len, hkv, g, dh = q.shape
    nb = s_len // BLK
    nc = kc.shape[1]
    nsel = s_len // L_SEL
    n_top = min(N_SEL, nsel)
    scale = dh ** -0.5
    c_start = jnp.arange(nc) * D_CMP
    c_end = c_start + L_CMP - 1
    j_start = jnp.arange(nsel) * L_SEL
    cover = ((c_start[:, None] < j_start[None, :] + L_SEL)
             & (c_end[:, None] >= j_start[None, :])).astype(jnp.float32)
    ks_t = ks.reshape(b, nsel, L_SEL, hkv, dh).transpose(0, 3, 1, 2, 4)
    vs_t = vs.reshape(b, nsel, L_SEL, hkv, dh).transpose(0, 3, 1, 2, 4)
    bi = jnp.arange(b)[:, None, None, None]
    hi = jnp.arange(hkv)[None, None, :, None]
    j = jnp.arange(nsel)

    def block(n):
        q0 = n * BLK
        qb = lax.dynamic_slice_in_dim(q, q0, BLK, axis=1)
        t = q0 + jnp.arange(BLK)
        cvalid = c_end[None, :] <= t[:, None]
        logits = (jnp.einsum('bqhgd,bchd->bhgqc', qb, kc).astype(jnp.float32) * scale
                  - slopes[None, :, :, None, None] * (t[:, None] - c_end[None, :]).astype(jnp.float32))
        p_cmp = jnp.where(cvalid, jax.nn.softmax(jnp.where(cvalid, logits, NEG), axis=-1), 0.0)
        o_cmp = jnp.einsum('bhgqc,bchd->bqhgd', p_cmp.astype(vc.dtype), vc)
        imp = jnp.einsum('bhgqc,cj->bqhj', p_cmp, cover)
        tb = (t // L_SEL)[None, :, None, None]
        forced = (j == 0) | (j == tb) | (j == tb - 1)
        imp = jnp.where(forced, jnp.inf, jnp.where(j > tb, -jnp.inf, imp))
        _, sel = lax.top_k(imp, n_top)
        ksel = ks_t[bi, hi, sel]
        vsel = vs_t[bi, hi, sel]
        spos = sel[..., None] * L_SEL + jnp.arange(L_SEL)
        sdist = (t[None, :, None, None, None] - spos)[:, :, :, None]
        logits_s = (jnp.einsum('bqhgd,bqhnld->bqhgnl', qb, ksel).astype(jnp.float32) * scale
                    - slopes[None, None, :, :, None, None] * sdist.astype(jnp.float32))
        logits_s = jnp.where(sdist >= 0, logits_s, NEG).reshape(b, BLK, hkv, g, n_top * L_SEL)
        p_slc = jax.nn.softmax(logits_s, axis=-1).reshape(b, BLK, hkv, g, n_top, L_SEL)
        o_slc = jnp.einsum('bqhgnl,bqhnld->bqhgd', p_slc.astype(vsel.dtype), vsel)
        return o_cmp, o_slc

    o_cmp, o_slc = lax.map(block, jnp.arange(nb))
    return unblock(o_cmp), unblock(o_slc)


def stick_breaking(q, k, v):
    b, s_len, h, dh = q.shape
    nb = s_len // BLK
    scale = dh ** -0.5

    def block(n):
        q0 = n * BLK
        qb = lax.dynamic_slice_in_dim(q, q0, BLK, axis=1)
        t = q0 + jnp.arange(BLK)

        def step(i, carry):
            o, acc = carry
            k0 = (n - i) * BLK
            kb = lax.dynamic_slice_in_dim(k, k0, BLK, axis=1)
            vb = lax.dynamic_slice_in_dim(v, k0, BLK, axis=1)
            s = k0 + jnp.arange(BLK)
            valid = s[None, :] < t[:, None]
            z = jnp.einsum('bqhd,bkhd->bhqk', qb, kb).astype(jnp.float32) * scale
            log_1mb = jnp.where(valid, jax.nn.log_sigmoid(-z), 0.0)
            later = lax.cumsum(log_1mb, axis=3, reverse=True) - log_1mb
            log_a = jax.nn.log_sigmoid(z) + later + acc[..., None]
            a = jnp.where(valid, jnp.exp(log_a), 0.0)
            o = o + jnp.einsum('bhqk,bkhd->bhqd', a, vb.astype(jnp.float32))
            return o, acc + jnp.sum(log_1mb, axis=-1)

        init = (jnp.zeros((b, h, BLK, dh), jnp.float32), jnp.zeros((b, h, BLK), jnp.float32))
        o, _ = lax.fori_loop(0, n + 1, step, init)
        return jnp.swapaxes(o, 1, 2).astype(v.dtype)

    return unblock(lax.map(block, jnp.arange(nb)))


def peer(x, w_q, sub_keys, u, v):
    b, s_len, d = x.shape
    nb = s_len // BLK

    def block(n):
        xb = lax.dynamic_slice_in_dim(x, n * BLK, BLK, axis=1)
        qr = jnp.einsum('btd,dhk->bthk', xb, w_q)
        s1 = jnp.einsum('bthk,hnk->bthn', qr[..., :PK_DIM], sub_keys[0]).astype(jnp.float32)
        s2 = jnp.einsum('bthk,hnk->bthn', qr[..., PK_DIM:], sub_keys[1]).astype(jnp.float32)
        v1, i1 = lax.top_k(s1, PEER_TOPK)
        v2, i2 = lax.top_k(s2, PEER_TOPK)
        cand = (v1[..., :, None] + v2[..., None, :]).reshape(b, BLK, PEER_HEADS, PEER_TOPK * PEER_TOPK)
        sc, ci = lax.top_k(cand, PEER_TOPK)
        e = (jnp.take_along_axis(i1, ci // PEER_TOPK, axis=-1) * N_KEYS
             + jnp.take_along_axis(i2, ci % PEER_TOPK, axis=-1))
        gate = jax.nn.softmax(sc, axis=-1)
        hid = jax.nn.gelu(jnp.einsum('bthkd,btd->bthk', u[e], xb).astype(jnp.float32)) * gate
        return jnp.einsum('bthk,bthkd->btd', hid.astype(x.dtype), v[e])

    return unblock(lax.map(block, jnp.arange(nb)))


def hybrid_layer(x, norm1, w_in, sinks, cmp_pos, cmp_w1, cmp_w2, w_branch, w_out,
                 norm2, peer_wq, peer_keys, peer_u, peer_v):
    b, s_len, _ = x.shape
    h = rms_norm(x, norm1)
    (qa, ka, va, qb, kcb, vcb, ksb, vsb, kwb, vwb, gb, qc, kc, vc, gm) = split_cols(h @ w_in, IN_WIDTHS)
    heads = lambda t_, n_: t_.reshape(b, s_len, n_, HEAD_DIM)
    groups = lambda t_, n_: t_.reshape(b, s_len, n_, -1, HEAD_DIM)

    slopes_a = alibi_slopes(A_HEADS).reshape(A_KV, A_GROUP)
    o_a = banded_gqa(groups(qa, A_KV), heads(ka, A_KV), heads(va, A_KV), A_WINDOW,
                     slopes_a, sinks.reshape(A_KV, A_GROUP))

    slopes_b = alibi_slopes(B_HEADS).reshape(B_KV, B_GROUP)
    qbh = groups(qb, B_KV)
    k_cmp = nsa_compress(heads(kcb, B_KV), cmp_pos[0], cmp_w1[0], cmp_w2[0])
    v_cmp = nsa_compress(heads(vcb, B_KV), cmp_pos[1], cmp_w1[1], cmp_w2[1])
    o_cmp, o_slc = nsa_compressed_selected(qbh, k_cmp, v_cmp, heads(ksb, B_KV), heads(vsb, B_KV), slopes_b)
    o_win = banded_gqa(qbh, heads(kwb, B_KV), heads(vwb, B_KV), B_WINDOW, slopes_b)
    g_b = jax.nn.sigmoid(gb.reshape(b, s_len, B_KV, B_GROUP, 3))
    o_b = g_b[..., 0:1] * o_cmp + g_b[..., 1:2] * o_slc + g_b[..., 2:3] * o_win

    o_c = stick_breaking(heads(qc, C_HEADS), heads(kc, C_HEADS), heads(vc, C_HEADS))

    y_a = o_a.reshape(b, s_len, MIX_W) @ w_branch[0]
    y_b = o_b.reshape(b, s_len, MIX_W) @ w_branch[1]
    y_c = o_c.reshape(b, s_len, MIX_W) @ w_branch[2]
    g_m = jax.nn.sigmoid(gm.reshape(b, s_len, 3, D_MODEL))
    mixed = g_m[:, :, 0] * y_a + g_m[:, :, 1] * y_b + g_m[:, :, 2] * y_c
    x = x + mixed @ w_out

    return x + peer(rms_norm(x, norm2), peer_wq, peer_keys, peer_u, peer_v)


def setup_inputs(seed: int = 0) -> dict:
    key = jax.random.key(seed)
    ks = jax.random.split(key, 16)
    nrm = lambda k_, shape, scale: jax.random.normal(k_, shape, jnp.float32) * scale
    return {
        "x": nrm(ks[0], (BATCH, SEQ, D_MODEL), 1.0),
        "norm1": 1.0 + nrm(ks[1], (DEPTH, D_MODEL), 0.01),
        "w_in": nrm(ks[2], (DEPTH, D_MODEL, D_IN), D_MODEL ** -0.5),
        "sinks": nrm(ks[3], (DEPTH, A_HEADS), 0.5),
        "cmp_pos": nrm(ks[4], (DEPTH, 2, L_CMP, HEAD_DIM), 0.02),
        "cmp_w1": nrm(ks[5], (DEPTH, 2, L_CMP, HEAD_DIM, CMP_HID), (L_CMP * HEAD_DIM) ** -0.5),
        "cmp_w2": nrm(ks[6], (DEPTH, 2, CMP_HID, HEAD_DIM), CMP_HID ** -0.5),
        "w_branch": nrm(ks[7], (DEPTH, 3, MIX_W, D_MODEL), MIX_W ** -0.5),
        "w_out": nrm(ks[8], (DEPTH, D_MODEL, D_MODEL), D_MODEL ** -0.5),
        "norm2": 1.0 + nrm(ks[9], (DEPTH, D_MODEL), 0.01),
        "peer_wq": nrm(ks[10], (DEPTH, D_MODEL, PEER_HEADS, 2 * PK_DIM), D_MODEL ** -0.5),
        "peer_keys": nrm(ks[11], (DEPTH, 2, PEER_HEADS, N_KEYS, PK_DIM), PK_DIM ** -0.5),
        "peer_u": nrm(ks[12], (DEPTH, N_EXPERTS, D_MODEL), D_MODEL ** -0.5),
        "peer_v": nrm(ks[13], (DEPTH, N_EXPERTS, D_MODEL), PEER_HEADS ** -0.5),
        "norm_f": 1.0 + nrm(ks[14], (D_MODEL,), 0.01),
    }


def reference(x, norm1, w_in, sinks, cmp_pos, cmp_w1, cmp_w2, w_branch, w_out,
              norm2, peer_wq, peer_keys, peer_u, peer_v, norm_f):
    for l in range(DEPTH):
        x = hybrid_layer(x, norm1[l], w_in[l], sinks[l], cmp_pos[l], cmp_w1[l], cmp_w2[l],
                         w_branch[l], w_out[l], norm2[l], peer_wq[l], peer_keys[l],
                         peer_u[l], peer_v[l])
    return rms_norm(x, norm_f)
```

```python
import functools

import jax
import jax.numpy as jnp
from jax import lax
from jax.experimental import pallas as pl
from jax.experimental.pallas import tpu as pltpu

F32 = jnp.float32
BF16 = jnp.bfloat16

LANES_V7X = 128
VMEM_BYTES_V7X = 64 * 1024 * 1024

D_MODEL = 1024
HEAD_DIM = 64
BLK = 128
N_HEADS = 8
N_KV = 2
GROUP = N_HEADS // N_KV
MIX_W = N_HEADS * HEAD_DIM
A_WINDOW = 128
B_WINDOW = 512
L_CMP = 32
D_CMP = 16
CMP_HID = 128
L_SEL = 64
N_SEL = 16
PEER_HEADS = 8
N_KEYS = 128
N_EXPERTS = N_KEYS * N_KEYS
PK_DIM = 128
PEER_TOPK = 16
RMS_EPS = 1e-6
NEG = -1e30
BIG = 3.0e38

IN_WIDTHS = (MIX_W, 128, 128, MIX_W, 128, 128, 128, 128, 128, 128, N_HEADS * 3,
             MIX_W, MIX_W, MIX_W, 3 * D_MODEL)

P_QC, P_KC, P_VC, P_QA, P_QB = 0, 512, 1024, 1536, 2048
P_KA, P_VA, P_KS, P_VS, P_KW, P_VW = 2560, 2816, 3072, 3328, 3584, 3840
P_KCB, P_VCB, P_GB = 4096, 4224, 4352
P_COLS = 4608


def _slope(h):
    return 2.0 ** (-(h + 1))


def _params(dims, vmem_mb):
    return pltpu.CompilerParams(dimension_semantics=dims,
                                vmem_limit_bytes=vmem_mb * 1024 * 1024)


def _resident(shape, index_map):
    return pl.BlockSpec(shape, index_map, pipeline_mode=pl.Buffered(1))


def _rms(x, g):
    ms = jnp.mean(x * x, axis=-1, keepdims=True)
    return x * lax.rsqrt(ms + RMS_EPS) * g


def _dot(a, b):
    return jnp.dot(a, b, preferred_element_type=F32)


def _dot_nt(a, b):
    return lax.dot_general(a, b, (((1,), (1,)), ((), ())), preferred_element_type=F32)


def _split_dot(x, w):
    hi = x.astype(BF16)
    lo = (x - hi.astype(F32)).astype(BF16)
    return _dot(hi, w) + _dot(lo, w)


def _gelu(x):
    return 0.5 * x * (1.0 + jnp.tanh(0.7978845608028654 * (x + 0.044715 * (x * x * x))))


def _head_q(q, h, lane):
    chunk = q[:, (h // 2) * LANES_V7X:(h // 2 + 1) * LANES_V7X]
    keep = (lane < HEAD_DIM) if h % 2 == 0 else (lane >= HEAD_DIM)
    return jnp.where(keep, chunk, jnp.zeros_like(chunk))


def _normproj_kernel(x_ref, g_ref, w_ref, o_ref, hn_ref):
    @pl.when(pl.program_id(1) == 0)
    def _():
        hn_ref[...] = _rms(x_ref[...], g_ref[...]).astype(BF16)

    o_ref[...] = _dot(hn_ref[...], w_ref[...]).astype(o_ref.dtype)


def _normproj(x, g, w, tm=1024, tn=1536):
    s, d = x.shape
    n = w.shape[1]
    return pl.pallas_call(
        _normproj_kernel,
        grid=(s // tm, n // tn),
        in_specs=[pl.BlockSpec((tm, d), lambda i, j: (i, 0)),
                  pl.BlockSpec((1, d), lambda i, j: (0, 0)),
                  pl.BlockSpec((d, tn), lambda i, j: (0, j))],
        out_specs=pl.BlockSpec((tm, tn), lambda i, j: (i, j)),
        out_shape=jax.ShapeDtypeStruct((s, n), BF16),
        scratch_shapes=[pltpu.VMEM((tm, d), BF16)],
        compiler_params=_params(("parallel", "arbitrary"), 48),
        name="normproj",
    )(x, g, w)


def _banded_kernel(*refs, nprev, use_sink):
    q_ref = refs[0]
    k_refs = refs[1:2 + nprev]
    v_refs = refs[2 + nprev:3 + 2 * nprev]
    pos = 3 + 2 * nprev
    if use_sink:
        sink_ref = refs[pos]
        pos += 1
    o_ref = refs[pos]

    n = pl.program_id(0)
    window = nprev * BLK
    row = lax.broadcasted_iota(jnp.int32, (BLK, BLK), 0)
    col = lax.broadcasted_iota(jnp.int32, (BLK, BLK), 1)
    q = q_ref[...]
    outs = []
    for h in range(N_HEADS):
        kv = h // GROUP
        qh = _head_q(q, h, col)
        logits = []
        for p in range(nprev + 1):
            dist = row - col + (nprev - p) * BLK
            s_pos = (n - nprev + p) * BLK + col
            valid = (s_pos >= 0) & (dist >= 0) & (dist < window)
            z = _dot_nt(qh, k_refs[p][:, kv * LANES_V7X:(kv + 1) * LANES_V7X])
            z = z - _slope(h) * dist.astype(F32)
            logits.append(jnp.where(valid, z, NEG))
        m = logits[0].max(axis=-1, keepdims=True)
        for lg in logits[1:]:
            m = jnp.maximum(m, lg.max(axis=-1, keepdims=True))
        if use_sink:
            sink = sink_ref[:, h:h + 1]
            m = jnp.maximum(m, sink)
            den = jnp.exp(sink - m)
        else:
            den = jnp.zeros_like(m)
        acc = jnp.zeros((BLK, LANES_V7X), F32)
        for p in range(nprev + 1):
            e = jnp.exp(logits[p] - m)
            den = den + e.sum(axis=-1, keepdims=True)
            acc = acc + _dot(e.astype(BF16), v_refs[p][:, kv * LANES_V7X:(kv + 1) * LANES_V7X])
        outs.append(acc / den)
    for j in range(N_HEADS // 2):
        o_ref[:, j * LANES_V7X:(j + 1) * LANES_V7X] = jnp.where(
            col < HEAD_DIM, outs[2 * j], outs[2 * j + 1]).astype(o_ref.dtype)


def _banded(p_all, q_off, k_off, v_off, window, sinks=None):
    s = p_all.shape[0]
    nprev = window // BLK
    kw = 2 * LANES_V7X

    def kv_spec(off, p):
        return pl.BlockSpec((BLK, kw), lambda n: (jnp.maximum(n - nprev + p, 0), off // kw))

    in_specs = [pl.BlockSpec((BLK, MIX_W), lambda n: (n, q_off // MIX_W))]
    in_specs += [kv_spec(k_off, p) for p in range(nprev + 1)]
    in_specs += [kv_spec(v_off, p) for p in range(nprev + 1)]
    args = [p_all] * (3 + 2 * nprev)
    if sinks is not None:
        in_specs.append(pl.BlockSpec((1, N_HEADS), lambda n: (0, 0)))
        args.append(sinks.reshape(1, N_HEADS))
    return pl.pallas_call(
        functools.partial(_banded_kernel, nprev=nprev, use_sink=sinks is not None),
        grid=(s // BLK,),
        in_specs=in_specs,
        out_specs=pl.BlockSpec((BLK, MIX_W), lambda n: (n, 0)),
        out_shape=jax.ShapeDtypeStruct((s, MIX_W), BF16),
        compiler_params=_params(("parallel",), 32),
        name="banded_w%d" % window,
    )(*args)


def _stick_kernel(q_ref, k_ref, v_ref, o_ref, acc_ref, r_ref):
    n = pl.program_id(0)
    row = lax.broadcasted_iota(jnp.int32, (BLK, BLK), 0)
    col = lax.broadcasted_iota(jnp.int32, (BLK, BLK), 1)
    suffix_ones = jnp.concatenate(
        [(row > col).astype(BF16), jnp.ones((BLK, BLK), BF16)], axis=1)
    q = q_ref[...]
    qh = [_head_q(q, h, col) for h in range(N_HEADS)]
    acc_ref[...] = jnp.zeros_like(acc_ref)
    r_ref[...] = jnp.zeros_like(r_ref)

    def tile(kb, diag):
        k0 = pl.multiple_of(kb * BLK, BLK)
        valid = col < row
        for j in range(N_HEADS // 2):
            kc = k_ref[pl.ds(k0, BLK), j * LANES_V7X:(j + 1) * LANES_V7X]
            vc = v_ref[pl.ds(k0, BLK), j * LANES_V7X:(j + 1) * LANES_V7X]
            for h in (2 * j, 2 * j + 1):
                z = _dot_nt(qh[h], kc)
                sp = jnp.maximum(z, 0.0) + jnp.log(1.0 + jnp.exp(-jnp.abs(z)))
                spm = jnp.where(valid, sp, 0.0) if diag else sp
                ct = _split_dot(spm, suffix_ones)
                log_a = z - sp - ct[:, :BLK] - acc_ref[h]
                a = jnp.exp(log_a)
                if diag:
                    a = jnp.where(valid, a, 0.0)
                r_ref[h] += _dot(a.astype(BF16), vc)
                acc_ref[h] += ct[:, BLK:]

    tile(n, True)

    def body(i, carry):
        tile(n - i, False)
        return carry

    lax.fori_loop(1, n + 1, body, 0)
    for j in range(N_HEADS // 2):
        o_ref[:, j * LANES_V7X:(j + 1) * LANES_V7X] = jnp.where(
            col < HEAD_DIM, r_ref[2 * j], r_ref[2 * j + 1]).astype(o_ref.dtype)


def _stick(p_all):
    s = p_all.shape[0]
    return pl.pallas_call(
        _stick_kernel,
        grid=(s // BLK,),
        in_specs=[pl.BlockSpec((BLK, MIX_W), lambda n: (n, P_QC // MIX_W)),
                  _resident((s, MIX_W), lambda n: (0, P_KC // MIX_W)),
                  _resident((s, MIX_W), lambda n: (0, P_VC // MIX_W))],
        out_specs=pl.BlockSpec((BLK, MIX_W), lambda n: (n, 0)),
        out_shape=jax.ShapeDtypeStruct((s, MIX_W), BF16),
        scratch_shapes=[pltpu.VMEM((N_HEADS, BLK, BLK), F32),
                        pltpu.VMEM((N_HEADS, BLK, LANES_V7X), F32)],
        compiler_params=_params(("parallel",), 48),
        name="stick_breaking",
    )(p_all, p_all, p_all)


def _compress_kernel(y_ref, w1_ref, pos_ref, w1f_ref, w2_ref, o_ref):
    nc = y_ref.shape[0]
    ab = _dot(y_ref[...], w1_ref[...])
    bias = _dot(pos_ref[...], w1f_ref[...])[0:1, :]
    for hh in range(N_KV):
        top = ab[:, hh * CMP_HID:(hh + 1) * CMP_HID]
        bot = ab[:, (N_KV + hh) * CMP_HID:(N_KV + hh + 1) * CMP_HID]
        hid = top + pltpu.roll(bot, nc - 1, 0) + bias
        o_ref[:, hh * LANES_V7X:(hh + 1) * LANES_V7X] = _dot(
            _gelu(hid).astype(BF16), w2_ref[...]).astype(o_ref.dtype)


def _compress(y, w1, pos, w2):
    nc = y.shape[0]
    half = L_CMP // 2
    zeros = jnp.zeros((half, HEAD_DIM, CMP_HID), F32)

    def place(w, hh):
        parts = [w, zeros] if hh == 0 else [zeros, w]
        return jnp.concatenate(parts, axis=1).reshape(half * 2 * HEAD_DIM, CMP_HID)

    w1cat = jnp.concatenate([place(w1[:half], 0), place(w1[:half], 1),
                             place(w1[half:], 0), place(w1[half:], 1)], axis=1).astype(BF16)
    posf = jnp.broadcast_to(pos.reshape(1, L_CMP * HEAD_DIM), (8, L_CMP * HEAD_DIM)).astype(BF16)
    w1f = w1.reshape(L_CMP * HEAD_DIM, CMP_HID).astype(BF16)
    w2d = jnp.concatenate([w2, w2], axis=1).astype(BF16)
    return pl.pallas_call(
        _compress_kernel,
        out_shape=jax.ShapeDtypeStruct((nc, 2 * LANES_V7X), BF16),
        compiler_params=pltpu.CompilerParams(vmem_limit_bytes=48 * 1024 * 1024),
        name="nsa_compress",
    )(y, w1cat, posf, w1f, w2d)


def _cmpsel_kernel(q_ref, kc_ref, vc_ref, cover_ref, o_ref, sel_ref):
    n = pl.program_id(0)
    nc = kc_ref.shape[0]
    nsel = cover_ref.shape[1]
    col = lax.broadcasted_iota(jnp.int32, (BLK, LANES_V7X), 1)
    t = n * BLK + lax.broadcasted_iota(jnp.int32, (BLK, nc), 0)
    c_end = lax.broadcasted_iota(jnp.int32, (BLK, nc), 1) * D_CMP + (L_CMP - 1)
    cvalid = c_end <= t
    cdist = (t - c_end).astype(F32)
    q = q_ref[...]
    outs = []
    for kv in range(N_KV):
        kc = kc_ref[:, kv * LANES_V7X:(kv + 1) * LANES_V7X]
        vc = vc_ref[:, kv * LANES_V7X:(kv + 1) * LANES_V7X]
        psum = jnp.zeros((BLK, nc), F32)
        for g in range(GROUP):
            h = kv * GROUP + g
            z = _dot_nt(_head_q(q, h, col), kc) - _slope(h) * cdist
            z = jnp.where(cvalid, z, NEG)
            e = jnp.exp(z - z.max(axis=-1, keepdims=True))
            p = jnp.where(cvalid, e / e.sum(axis=-1, keepdims=True), 0.0)
            outs.append(_dot(p.astype(BF16), vc))
            psum = psum + p
        imp = _split_dot(psum, cover_ref[...])
        jj = lax.broadcasted_iota(jnp.int32, (BLK, nsel), 1)
        tb = (n * BLK + lax.broadcasted_iota(jnp.int32, (BLK, nsel), 0)) // L_SEL
        forced = (jj == 0) | (jj == tb) | (jj == tb - 1)
        cur = jnp.where(forced, BIG, jnp.where(jj > tb, -BIG, imp))
        picked = jnp.zeros((BLK, nsel), jnp.bool_)
        for _ in range(min(N_SEL, nsel)):
            m = cur.max(axis=-1, keepdims=True)
            first = jnp.where(cur == m, jj, nsel).min(axis=-1, keepdims=True)
            hit = jj == first
            picked = picked | hit
            cur = jnp.where(hit, -BIG, cur)
        sel_ref[:, kv * nsel:(kv + 1) * nsel] = jnp.where(picked, 1.0, 0.0).astype(sel_ref.dtype)
    for j in range(N_HEADS // 2):
        o_ref[:, j * LANES_V7X:(j + 1) * LANES_V7X] = jnp.where(
            col < HEAD_DIM, outs[2 * j], outs[2 * j + 1]).astype(o_ref.dtype)


def _cmpsel(p_all, kcmp, vcmp, cover):
    s = p_all.shape[0]
    nc, nsel = cover.shape
    return pl.pallas_call(
        _cmpsel_kernel,
        grid=(s // BLK,),
        in_specs=[pl.BlockSpec((BLK, MIX_W), lambda n: (n, P_QB // MIX_W)),
                  pl.BlockSpec((nc, 2 * LANES_V7X), lambda n: (0, 0)),
                  pl.BlockSpec((nc, 2 * LANES_V7X), lambda n: (0, 0)),
                  pl.BlockSpec((nc, nsel), lambda n: (0, 0))],
        out_specs=[pl.BlockSpec((BLK, MIX_W), lambda n: (n, 0)),
                   pl.BlockSpec((BLK, N_KV * nsel), lambda n: (n, 0))],
        out_shape=[jax.ShapeDtypeStruct((s, MIX_W), BF16),
                   jax.ShapeDtypeStruct((s, N_KV * nsel), BF16)],
        compiler_params=_params(("parallel",), 48),
        name="nsa_cmp_select",
    )(p_all, kcmp, vcmp, cover)


def _selattn_kernel(q_ref, sel_ref, k_ref, v_ref, o_ref, m_ref, l_ref, r_ref):
    n = pl.program_id(0)
    nsel = sel_ref.shape[1] // N_KV
    row = lax.broadcasted_iota(jnp.int32, (BLK, BLK), 0)
    col = lax.broadcasted_iota(jnp.int32, (BLK, BLK), 1)
    q = q_ref[...]
    qh = [_head_q(q, h, col) for h in range(N_HEADS)]
    m_ref[...] = jnp.full_like(m_ref, NEG)
    l_ref[...] = jnp.zeros_like(l_ref)
    r_ref[...] = jnp.zeros_like(r_ref)
    ej = lax.broadcasted_iota(jnp.int32, (nsel, BLK), 0)
    ek = lax.broadcasted_iota(jnp.int32, (nsel, BLK), 1) // L_SEL

    def body(c, carry):
        k0 = pl.multiple_of(c * BLK, BLK)
        dist = (n - c) * BLK + row - col
        causal = dist >= 0
        distf = dist.astype(F32)
        expand = jnp.where(ej == c * (BLK // L_SEL) + ek, 1.0, 0.0).astype(BF16)
        for kv in range(N_KV):
            chosen = _dot(sel_ref[:, kv * nsel:(kv + 1) * nsel], expand) > 0.5
            valid = chosen & causal
            kc = k_ref[pl.ds(k0, BLK), kv * LANES_V7X:(kv + 1) * LANES_V7X]
            vc = v_ref[pl.ds(k0, BLK), kv * LANES_V7X:(kv + 1) * LANES_V7X]
            for g in range(GROUP):
                h = kv * GROUP + g
                z = _dot_nt(qh[h], kc) - _slope(h) * distf
                z = jnp.where(valid, z, NEG)
                m_old = m_ref[h]
                m_new = jnp.maximum(m_old, z.max(axis=-1, keepdims=True))
                alpha = jnp.exp(m_old - m_new)
                e = jnp.where(valid, jnp.exp(z - m_new), 0.0)
                l_ref[h] = alpha * l_ref[h] + e.sum(axis=-1, keepdims=True)
                r_ref[h] = alpha * r_ref[h] + _dot(e.astype(BF16), vc)
                m_ref[h] = m_new
        return carry

    lax.fori_loop(0, n + 1, body, 0)
    outs = [r_ref[h] / l_ref[h] for h in range(N_HEADS)]
    for j in range(N_HEADS // 2):
        o_ref[:, j * LANES_V7X:(j + 1) * LANES_V7X] = jnp.where(
            col < HEAD_DIM, outs[2 * j], outs[2 * j + 1]).astype(o_ref.dtype)


def _selattn(p_all, sel):
    s = p_all.shape[0]
    kw = 2 * LANES_V7X
    return pl.pallas_call(
        _selattn_kernel,
        grid=(s // BLK,),
        in_specs=[pl.BlockSpec((BLK, MIX_W), lambda n: (n, P_QB // MIX_W)),
                  pl.BlockSpec((BLK, sel.shape[1]), lambda n: (n, 0)),
                  _resident((s, kw), lambda n: (0, P_KS // kw)),
                  _resident((s, kw), lambda n: (0, P_VS // kw))],
        out_specs=pl.BlockSpec((BLK, MIX_W), lambda n: (n, 0)),
        out_shape=jax.ShapeDtypeStruct((s, MIX_W), BF16),
        scratch_shapes=[pltpu.VMEM((N_HEADS, BLK, 1), F32),
                        pltpu.VMEM((N_HEADS, BLK, 1), F32),
                        pltpu.VMEM((N_HEADS, BLK, LANES_V7X), F32)],
        compiler_params=_params(("parallel",), 48),
        name="nsa_selected",
    )(p_all, sel, p_all, p_all)


def _merge_kernel(x_ref, g1_ref, g2_ref, oa_ref, ocmp_ref, oslc_ref, owin_ref, oc_ref, gb_ref,
                  gexp_ref, wgm_ref, wbr_ref, wout_ref, xo_ref, hn_ref):
    x = x_ref[...]
    hn = _rms(x, g1_ref[...]).astype(BF16)
    gb = jax.nn.sigmoid(gb_ref[...].astype(F32))
    gexp = _split_dot(gb, gexp_ref[...])
    ob = (gexp[:, 0:MIX_W] * ocmp_ref[...].astype(F32)
          + gexp[:, MIX_W:2 * MIX_W] * oslc_ref[...].astype(F32)
          + gexp[:, 2 * MIX_W:3 * MIX_W] * owin_ref[...].astype(F32)).astype(BF16)
    mixed = jnp.zeros(x.shape, F32)
    for i, o in enumerate((oa_ref[...], ob, oc_ref[...])):
        gm = jax.nn.sigmoid(_dot(hn, wgm_ref[:, i * D_MODEL:(i + 1) * D_MODEL]))
        mixed = mixed + gm * _dot(o, wbr_ref[i])
    xn = x + _dot(mixed.astype(BF16), wout_ref[...])
    xo_ref[...] = xn
    hn_ref[...] = _rms(xn, g2_ref[...]).astype(BF16)


def _merge(x, g1, g2, oa, ocmp, oslc, owin, oc, p_all, gexp, wgm, wbr, wout, tm=256):
    s, d = x.shape
    row = lambda w: pl.BlockSpec((tm, w), lambda i: (i, 0))
    return pl.pallas_call(
        _merge_kernel,
        grid=(s // tm,),
        in_specs=[row(d),
                  pl.BlockSpec((1, d), lambda i: (0, 0)),
                  pl.BlockSpec((1, d), lambda i: (0, 0)),
                  row(MIX_W), row(MIX_W), row(MIX_W), row(MIX_W), row(MIX_W),
                  pl.BlockSpec((tm, LANES_V7X), lambda i: (i, P_GB // LANES_V7X)),
                  _resident(gexp.shape, lambda i: (0, 0)),
                  _resident(wgm.shape, lambda i: (0, 0)),
                  _resident(wbr.shape, lambda i: (0, 0, 0)),
                  _resident(wout.shape, lambda i: (0, 0))],
        out_specs=[row(d), row(d)],
        out_shape=[jax.ShapeDtypeStruct((s, d), F32), jax.ShapeDtypeStruct((s, d), BF16)],
        compiler_params=_params(("parallel",), 48),
        name="merge",
    )(x, g1, g2, oa, ocmp, oslc, owin, oc, p_all, gexp, wgm, wbr, wout)


def _top_values(x, out_ref):
    rows = lax.broadcasted_iota(jnp.int32, x.shape, 0)
    nrow = x.shape[0]

    def body(r, cur):
        m = cur.max(axis=0, keepdims=True)
        out_ref[pl.ds(r, 1), :] = m
        first = jnp.where(cur == m, rows, nrow).min(axis=0, keepdims=True)
        return jnp.where(rows == first, -BIG, cur)

    lax.fori_loop(0, PEER_TOPK, body, x)


def _route_kernel(xn_ref, wq_ref, keys_ref, s1_ref, s2_ref, e1_ref, e2_ref, tau_ref,
                  qr_ref, v1_ref, v2_ref, sc_ref):
    qr_ref[...] = _dot_nt(wq_ref[...], xn_ref[...]).astype(BF16)
    for h in range(PEER_HEADS):
        base = h * 2 * PK_DIM
        s1 = _dot(keys_ref[0, h], qr_ref[base:base + PK_DIM, :])
        s2 = _dot(keys_ref[1, h], qr_ref[base + PK_DIM:base + 2 * PK_DIM, :])
        s1_ref[h] = s1
        s2_ref[h] = s2
        _top_values(s1, v1_ref)
        _top_values(s2, v2_ref)
        v2 = v2_ref[...]
        cand = jnp.concatenate([v1_ref[a:a + 1, :] + v2 for a in range(PEER_TOPK)], axis=0)
        _top_values(cand, sc_ref)
        sc = sc_ref[...]
        z = jnp.exp(sc - sc[0:1, :]).sum(axis=0, keepdims=True)
        tau_ref[h:h + 1, :] = sc[PEER_TOPK - 1:PEER_TOPK, :]
        e1_ref[h] = jnp.exp(s1 - v1_ref[0:1, :]) / z
        e2_ref[h] = jnp.exp(s2 - v2_ref[0:1, :])


def _route(xn, wq_t, keys, tm=256):
    s, d = xn.shape
    big = lambda: pl.BlockSpec((PEER_HEADS, N_KEYS, tm), lambda i: (0, 0, i))
    big_shape = jax.ShapeDtypeStruct((PEER_HEADS, N_KEYS, s), F32)
    return pl.pallas_call(
        _route_kernel,
        grid=(s // tm,),
        in_specs=[pl.BlockSpec((tm, d), lambda i: (i, 0)),
                  _resident(wq_t.shape, lambda i: (0, 0)),
                  _resident(keys.shape, lambda i: (0, 0, 0, 0))],
        out_specs=[big(), big(), big(), big(),
                   pl.BlockSpec((PEER_HEADS, tm), lambda i: (0, i))],
        out_shape=[big_shape, big_shape, big_shape, big_shape,
                   jax.ShapeDtypeStruct((PEER_HEADS, s), F32)],
        scratch_shapes=[pltpu.VMEM((PEER_HEADS * 2 * PK_DIM, tm), BF16),
                        pltpu.VMEM((PEER_TOPK, tm), F32),
                        pltpu.VMEM((PEER_TOPK, tm), F32),
                        pltpu.VMEM((PEER_TOPK, tm), F32)],
        compiler_params=_params(("parallel",), 48),
        name="peer_route",
    )(xn, wq_t, keys)


def _experts_kernel(x_ref, xn_ref, u_ref, vt_ref, s1_ref, s2_ref, e1_ref, e2_ref, tau_ref,
                    o_ref, acc_ref, *, chunks):
    j = pl.program_id(1)

    @pl.when(j == 0)
    def _():
        acc_ref[...] = jnp.zeros_like(acc_ref)

    hid = _gelu(_dot_nt(u_ref[...], xn_ref[...]))
    parts = []
    for c in range(chunks):
        i1 = j * chunks + c
        w = jnp.zeros((N_KEYS, hid.shape[1]), F32)
        for h in range(PEER_HEADS):
            score = s1_ref[h, pl.ds(i1, 1), :] + s2_ref[h]
            gate = e1_ref[h, pl.ds(i1, 1), :] * e2_ref[h]
            w = w + jnp.where(score >= tau_ref[h:h + 1, :], gate, 0.0)
        parts.append((w * hid[c * N_KEYS:(c + 1) * N_KEYS, :]).astype(BF16))
    acc_ref[...] += _dot(vt_ref[...], jnp.concatenate(parts, axis=0))

    @pl.when(j == pl.num_programs(1) - 1)
    def _():
        o_ref[...] = x_ref[...] + acc_ref[...].T


def _experts(x, xn, u, vt, s1, s2, e1, e2, tau, tm=512, te=512):
    s, d = x.shape
    big = lambda: pl.BlockSpec((PEER_HEADS, N_KEYS, tm), lambda i, j: (0, 0, i))
    return pl.pallas_call(
        functools.partial(_experts_kernel, chunks=te // N_KEYS),
        grid=(s // tm, N_EXPERTS // te),
        in_specs=[pl.BlockSpec((tm, d), lambda i, j: (i, 0)),
                  pl.BlockSpec((tm, d), lambda i, j: (i, 0)),
                  pl.BlockSpec((te, d), lambda i, j: (j, 0)),
                  pl.BlockSpec((d, te), lambda i, j: (0, j)),
                  big(), big(), big(), big(),
                  pl.BlockSpec((PEER_HEADS, tm), lambda i, j: (0, i))],
        out_specs=pl.BlockSpec((tm, d), lambda i, j: (i, 0)),
        out_shape=jax.ShapeDtypeStruct((s, d), F32),
        scratch_shapes=[pltpu.VMEM((d, tm), F32)],
        compiler_params=_params(("parallel", "arbitrary"), 56),
        name="peer_experts",
    )(x, xn, u, vt, s1, s2, e1, e2, tau)


def _norm_kernel(x_ref, g_ref, o_ref):
    o_ref[...] = _rms(x_ref[...], g_ref[...])


def _final_norm(x, g, tm=1024):
    s, d = x.shape
    return pl.pallas_call(
        _norm_kernel,
        grid=(s // tm,),
        in_specs=[pl.BlockSpec((tm, d), lambda i: (i, 0)), pl.BlockSpec((1, d), lambda i: (0, 0))],
        out_specs=pl.BlockSpec((tm, d), lambda i: (i, 0)),
        out_shape=jax.ShapeDtypeStruct((s, d), F32),
        compiler_params=_params(("parallel",), 32),
        name="final_norm",
    )(x, g)


def _dup_heads(w):
    a, b = w[:, :HEAD_DIM], w[:, HEAD_DIM:]
    return jnp.concatenate([a, a, b, b], axis=1)


def _pack_w_in(w_in):
    parts, start = [], 0
    for w in IN_WIDTHS:
        parts.append(w_in[:, start:start + w])
        start += w
    qa, ka, va, qb, kcb, vcb, ksb, vsb, kwb, vwb, gb, qc, kc, vc, gm = parts
    scale = HEAD_DIM ** -0.5
    d = w_in.shape[0]
    packed = jnp.concatenate(
        [qc * scale, kc, vc, qa * scale, qb * scale,
         _dup_heads(ka), _dup_heads(va), _dup_heads(ksb), _dup_heads(vsb),
         _dup_heads(kwb), _dup_heads(vwb), kcb, vcb,
         gb, jnp.zeros((d, P_COLS - P_GB - gb.shape[1]), F32)], axis=1)
    return packed.astype(BF16), gm.astype(BF16)


def _gate_expand():
    r = jnp.arange(LANES_V7X)[:, None]
    c = jnp.arange(3 * MIX_W)[None, :]
    return ((r == 3 * ((c % MIX_W) // HEAD_DIM) + c // MIX_W) & (r < 3 * N_HEADS)).astype(BF16)


def _cover(nc, nsel):
    c_start = jnp.arange(nc)[:, None] * D_CMP
    j_start = jnp.arange(nsel)[None, :] * L_SEL
    real = jnp.arange(nc)[:, None] < nc - 1
    return (real & (c_start < j_start + L_SEL) & (c_start + L_CMP - 1 >= j_start)).astype(BF16)


def _layer(x, norm1, w_in, sinks, cmp_pos, cmp_w1, cmp_w2, w_branch, w_out,
           norm2, peer_wq, peer_keys, peer_u, peer_v):
    s = x.shape[0]
    nc = s // D_CMP
    w_pack, w_gm = _pack_w_in(w_in)
    p_all = _normproj(x, norm1.reshape(1, -1), w_pack)

    o_a = _banded(p_all, P_QA, P_KA, P_VA, A_WINDOW, sinks)
    o_c = _stick(p_all)

    y_k = p_all[:, P_KCB:P_KCB + LANES_V7X].reshape(nc, D_CMP * LANES_V7X)
    y_v = p_all[:, P_VCB:P_VCB + LANES_V7X].reshape(nc, D_CMP * LANES_V7X)
    k_cmp = _compress(y_k, cmp_w1[0], cmp_pos[0], cmp_w2[0])
    v_cmp = _compress(y_v, cmp_w1[1], cmp_pos[1], cmp_w2[1])
    o_cmp, sel = _cmpsel(p_all, k_cmp, v_cmp, _cover(nc, s // L_SEL))
    o_slc = _selattn(p_all, sel)
    o_win = _banded(p_all, P_QB, P_KW, P_VW, B_WINDOW)

    x, xn = _merge(x, norm1.reshape(1, -1), norm2.reshape(1, -1), o_a, o_cmp, o_slc, o_win, o_c,
                   p_all, _gate_expand(), w_gm, w_branch.astype(BF16), w_out.astype(BF16))

    wq_t = peer_wq.reshape(D_MODEL, PEER_HEADS * 2 * PK_DIM).T.astype(BF16)
    s1, s2, e1, e2, tau = _route(xn, wq_t, peer_keys.astype(BF16))
    return _experts(x, xn, peer_u.astype(BF16), peer_v.T.astype(BF16), s1, s2, e1, e2, tau)


def kernel(x, norm1, w_in, sinks, cmp_pos, cmp_w1, cmp_w2, w_branch, w_out, norm2, peer_wq,
           peer_keys, peer_u, peer_v, norm_f):
    b, s, d = x.shape
    assert b == 1 and d == D_MODEL and s % (8 * BLK) == 0
    h = x.reshape(s, d)
    for l in range(norm1.shape[0]):
        h = _layer(h, norm1[l], w_in[l], sinks[l], cmp_pos[l], cmp_w1[l], cmp_w2[l],
                   w_branch[l], w_out[l], norm2[l], peer_wq[l], peer_keys[l],
                   peer_u[l], peer_v[l])
    return _final_norm(h, norm_f.reshape(1, -1)).reshape(b, s, d)
```

```python
import functools

import jax
import jax.numpy as jnp
from jax import lax
from jax.experimental import pallas as pl
from jax.experimental.pallas import tpu as pltpu

F32 = jnp.float32
BF16 = jnp.bfloat16

LANES_V7X = 128
VMEM_BYTES_V7X = 64 * 1024 * 1024

D_MODEL = 1024
HEAD_DIM = 64
BLK = 128
N_HEADS = 8
N_KV = 2
GROUP = N_HEADS // N_KV
MIX_W = N_HEADS * HEAD_DIM
A_WINDOW = 128
B_WINDOW = 512
L_CMP = 32
D_CMP = 16
CMP_HID = 128
L_SEL = 64
N_SEL = 16
PEER_HEADS = 8
N_KEYS = 128
N_EXPERTS = N_KEYS * N_KEYS
PK_DIM = 128
PEER_TOPK = 16
RMS_EPS = 1e-6
NEG = -1e30
BIG = 3.0e38
STICK_DONE = 110.0

IN_WIDTHS = (MIX_W, 128, 128, MIX_W, 128, 128, 128, 128, 128, 128, N_HEADS * 3,
             MIX_W, MIX_W, MIX_W, 3 * D_MODEL)

P_QC, P_KC, P_VC, P_QA, P_QB = 0, 512, 1024, 1536, 2048
P_KA, P_VA, P_KS, P_VS, P_KW, P_VW = 2560, 2816, 3072, 3328, 3584, 3840
P_KCB, P_VCB, P_GB = 4096, 4224, 4352
P_COLS = 4608


def _slope(h):
    return 2.0 ** (-(h + 1))


def _params(dims, vmem_mb):
    return pltpu.CompilerParams(dimension_semantics=dims,
                                vmem_limit_bytes=vmem_mb * 1024 * 1024)


def _resident(shape, index_map):
    return pl.BlockSpec(shape, index_map, pipeline_mode=pl.Buffered(1))


def _rms(x, g):
    ms = jnp.mean(x * x, axis=-1, keepdims=True)
    return x * lax.rsqrt(ms + RMS_EPS) * g


def _dot(a, b):
    return jnp.dot(a, b, preferred_element_type=F32)


def _dot_nt(a, b):
    return lax.dot_general(a, b, (((1,), (1,)), ((), ())), preferred_element_type=F32)


def _split_dot(x, w):
    hi = x.astype(BF16)
    lo = (x - hi.astype(F32)).astype(BF16)
    return _dot(hi, w) + _dot(lo, w)


def _gelu(x):
    return 0.5 * x * (1.0 + jnp.tanh(0.7978845608028654 * (x + 0.044715 * (x * x * x))))


def _head_q(q, h, lane):
    chunk = q[:, (h // 2) * LANES_V7X:(h // 2 + 1) * LANES_V7X]
    keep = (lane < HEAD_DIM) if h % 2 == 0 else (lane >= HEAD_DIM)
    return jnp.where(keep, chunk, jnp.zeros_like(chunk))


def _normproj_kernel(x_ref, g_ref, w_ref, o_ref, hn_ref):
    @pl.when(pl.program_id(1) == 0)
    def _():
        hn_ref[...] = _rms(x_ref[...], g_ref[...]).astype(BF16)

    o_ref[...] = _dot(hn_ref[...], w_ref[...]).astype(o_ref.dtype)


def _normproj(x, g, w, tm=1024, tn=1536):
    s, d = x.shape
    n = w.shape[1]
    return pl.pallas_call(
        _normproj_kernel,
        grid=(s // tm, n // tn),
        in_specs=[pl.BlockSpec((tm, d), lambda i, j: (i, 0)),
                  pl.BlockSpec((1, d), lambda i, j: (0, 0)),
                  pl.BlockSpec((d, tn), lambda i, j: (0, j))],
        out_specs=pl.BlockSpec((tm, tn), lambda i, j: (i, j)),
        out_shape=jax.ShapeDtypeStruct((s, n), BF16),
        scratch_shapes=[pltpu.VMEM((tm, d), BF16)],
        compiler_params=_params(("parallel", "arbitrary"), 48),
        name="normproj",
    )(x, g, w)


def _banded_kernel(*refs, nprev, use_sink):
    q_ref = refs[0]
    k_refs = refs[1:2 + nprev]
    v_refs = refs[2 + nprev:3 + 2 * nprev]
    pos = 3 + 2 * nprev
    if use_sink:
        sink_ref = refs[pos]
        pos += 1
    o_ref = refs[pos]

    n = pl.program_id(0)
    window = nprev * BLK
    row = lax.broadcasted_iota(jnp.int32, (BLK, BLK), 0)
    col = lax.broadcasted_iota(jnp.int32, (BLK, BLK), 1)
    q = q_ref[...]
    outs = []
    for h in range(N_HEADS):
        kv = h // GROUP
        qh = _head_q(q, h, col)
        logits = []
        for p in range(nprev + 1):
            dist = row - col + (nprev - p) * BLK
            s_pos = (n - nprev + p) * BLK + col
            valid = (s_pos >= 0) & (dist >= 0) & (dist < window)
            z = _dot_nt(qh, k_refs[p][:, kv * LANES_V7X:(kv + 1) * LANES_V7X])
            z = z - _slope(h) * dist.astype(F32)
            logits.append(jnp.where(valid, z, NEG))
        m = logits[0].max(axis=-1, keepdims=True)
        for lg in logits[1:]:
            m = jnp.maximum(m, lg.max(axis=-1, keepdims=True))
        if use_sink:
            sink = sink_ref[:, h:h + 1]
            m = jnp.maximum(m, sink)
            den = jnp.exp(sink - m)
        else:
            den = jnp.zeros_like(m)
        acc = jnp.zeros((BLK, LANES_V7X), F32)
        for p in range(nprev + 1):
            e = jnp.exp(logits[p] - m)
            den = den + e.sum(axis=-1, keepdims=True)
            acc = acc + _dot(e.astype(BF16), v_refs[p][:, kv * LANES_V7X:(kv + 1) * LANES_V7X])
        outs.append(acc / den)
    for j in range(N_HEADS // 2):
        o_ref[:, j * LANES_V7X:(j + 1) * LANES_V7X] = jnp.where(
            col < HEAD_DIM, outs[2 * j], outs[2 * j + 1]).astype(o_ref.dtype)


def _banded(p_all, q_off, k_off, v_off, window, sinks=None):
    s = p_all.shape[0]
    nprev = window // BLK
    kw = 2 * LANES_V7X

    def kv_spec(off, p):
        return pl.BlockSpec((BLK, kw), lambda n: (jnp.maximum(n - nprev + p, 0), off // kw))

    in_specs = [pl.BlockSpec((BLK, MIX_W), lambda n: (n, q_off // MIX_W))]
    in_specs += [kv_spec(k_off, p) for p in range(nprev + 1)]
    in_specs += [kv_spec(v_off, p) for p in range(nprev + 1)]
    args = [p_all] * (3 + 2 * nprev)
    if sinks is not None:
        in_specs.append(pl.BlockSpec((1, N_HEADS), lambda n: (0, 0)))
        args.append(sinks.reshape(1, N_HEADS))
    return pl.pallas_call(
        functools.partial(_banded_kernel, nprev=nprev, use_sink=sinks is not None),
        grid=(s // BLK,),
        in_specs=in_specs,
        out_specs=pl.BlockSpec((BLK, MIX_W), lambda n: (n, 0)),
        out_shape=jax.ShapeDtypeStruct((s, MIX_W), BF16),
        compiler_params=_params(("parallel",), 32),
        name="banded_w%d" % window,
    )(*args)


def _stick_kernel(q_ref, k_ref, v_ref, o_ref, q8_ref, acc_ref, r_ref):
    n = pl.program_id(0)
    rows = N_HEADS * BLK
    pair = 2 * BLK
    row = lax.broadcasted_iota(jnp.int32, (BLK, BLK), 0)
    col = lax.broadcasted_iota(jnp.int32, (BLK, BLK), 1)
    suffix_ones = jnp.concatenate(
        [(row > col).astype(BF16), jnp.ones((BLK, BLK), BF16)], axis=1)
    q = q_ref[...]
    for h in range(N_HEADS):
        q8_ref[h * BLK:(h + 1) * BLK, :] = _head_q(q, h, col)
    acc_ref[...] = jnp.zeros_like(acc_ref)
    r_ref[...] = jnp.zeros_like(r_ref)

    def tile(kb, diag):
        k0 = pl.multiple_of(kb * BLK, BLK)
        z = jnp.concatenate(
            [_dot_nt(q8_ref[j * pair:(j + 1) * pair, :],
                     k_ref[pl.ds(k0, BLK), j * LANES_V7X:(j + 1) * LANES_V7X])
             for j in range(N_HEADS // 2)], axis=0)
        sp = jnp.maximum(z, 0.0) + jnp.log(1.0 + jnp.exp(-jnp.abs(z)))
        if diag:
            t_in = lax.broadcasted_iota(jnp.int32, (rows, BLK), 0) & (BLK - 1)
            valid = lax.broadcasted_iota(jnp.int32, (rows, BLK), 1) < t_in
            spm = jnp.where(valid, sp, 0.0)
        else:
            spm = sp
        ct = _split_dot(spm, suffix_ones)
        a = jnp.exp(z - sp - ct[:, :BLK] - acc_ref[...])
        if diag:
            a = jnp.where(valid, a, 0.0)
        a = a.astype(BF16)
        for j in range(N_HEADS // 2):
            r_ref[j * pair:(j + 1) * pair, :] += _dot(
                a[j * pair:(j + 1) * pair, :],
                v_ref[pl.ds(k0, BLK), j * LANES_V7X:(j + 1) * LANES_V7X])
        acc_ref[...] += ct[:, BLK:]
        return jnp.min(acc_ref[...])

    def cond(carry):
        i, acc_min = carry
        return (i <= n) & (acc_min < STICK_DONE)

    def body(carry):
        i, _ = carry
        return i + 1, tile(n - i, False)

    lax.while_loop(cond, body, (jnp.int32(1), tile(n, True)))
    for j in range(N_HEADS // 2):
        o_ref[:, j * LANES_V7X:(j + 1) * LANES_V7X] = jnp.where(
            col < HEAD_DIM, r_ref[2 * j * BLK:(2 * j + 1) * BLK, :],
            r_ref[(2 * j + 1) * BLK:(2 * j + 2) * BLK, :]).astype(o_ref.dtype)


def _stick(p_all):
    s = p_all.shape[0]
    return pl.pallas_call(
        _stick_kernel,
        grid=(s // BLK,),
        in_specs=[pl.BlockSpec((BLK, MIX_W), lambda n: (n, P_QC // MIX_W)),
                  _resident((s, MIX_W), lambda n: (0, P_KC // MIX_W)),
                  _resident((s, MIX_W), lambda n: (0, P_VC // MIX_W))],
        out_specs=pl.BlockSpec((BLK, MIX_W), lambda n: (n, 0)),
        out_shape=jax.ShapeDtypeStruct((s, MIX_W), BF16),
        scratch_shapes=[pltpu.VMEM((N_HEADS * BLK, LANES_V7X), BF16),
                        pltpu.VMEM((N_HEADS * BLK, BLK), F32),
                        pltpu.VMEM((N_HEADS * BLK, LANES_V7X), F32)],
        compiler_params=_params(("parallel",), 48),
        name="stick_breaking",
    )(p_all, p_all, p_all)


def _compress_kernel(y_ref, w1_ref, pos_ref, w1f_ref, w2_ref, o_ref):
    nc = y_ref.shape[0]
    ab = _dot(y_ref[...], w1_ref[...])
    bias = _dot(pos_ref[...], w1f_ref[...])[0:1, :]
    for hh in range(N_KV):
        top = ab[:, hh * CMP_HID:(hh + 1) * CMP_HID]
        bot = ab[:, (N_KV + hh) * CMP_HID:(N_KV + hh + 1) * CMP_HID]
        hid = top + pltpu.roll(bot, nc - 1, 0) + bias
        o_ref[:, hh * LANES_V7X:(hh + 1) * LANES_V7X] = _dot(
            _gelu(hid).astype(BF16), w2_ref[...]).astype(o_ref.dtype)


def _compress(y, w1, pos, w2):
    nc = y.shape[0]
    half = L_CMP // 2
    zeros = jnp.zeros((half, HEAD_DIM, CMP_HID), F32)

    def place(w, hh):
        parts = [w, zeros] if hh == 0 else [zeros, w]
        return jnp.concatenate(parts, axis=1).reshape(half * 2 * HEAD_DIM, CMP_HID)

    w1cat = jnp.concatenate([place(w1[:half], 0), place(w1[:half], 1),
                             place(w1[half:], 0), place(w1[half:], 1)], axis=1).astype(BF16)
    posf = jnp.broadcast_to(pos.reshape(1, L_CMP * HEAD_DIM), (8, L_CMP * HEAD_DIM)).astype(BF16)
    w1f = w1.reshape(L_CMP * HEAD_DIM, CMP_HID).astype(BF16)
    w2d = jnp.concatenate([w2, w2], axis=1).astype(BF16)
    return pl.pallas_call(
        _compress_kernel,
        out_shape=jax.ShapeDtypeStruct((nc, 2 * LANES_V7X), BF16),
        compiler_params=pltpu.CompilerParams(vmem_limit_bytes=48 * 1024 * 1024),
        name="nsa_compress",
    )(y, w1cat, posf, w1f, w2d)


def _cmpsel_kernel(q_ref, kc_ref, vc_ref, cover_ref, o_ref, sel_ref):
    n = pl.program_id(0)
    nc = kc_ref.shape[0]
    nsel = cover_ref.shape[1]
    col = lax.broadcasted_iota(jnp.int32, (BLK, LANES_V7X), 1)
    t = n * BLK + lax.broadcasted_iota(jnp.int32, (BLK, nc), 0)
    c_end = lax.broadcasted_iota(jnp.int32, (BLK, nc), 1) * D_CMP + (L_CMP - 1)
    cvalid = c_end <= t
    cdist = (t - c_end).astype(F32)
    q = q_ref[...]
    outs = []
    for kv in range(N_KV):
        kc = kc_ref[:, kv * LANES_V7X:(kv + 1) * LANES_V7X]
        vc = vc_ref[:, kv * LANES_V7X:(kv + 1) * LANES_V7X]
        psum = jnp.zeros((BLK, nc), F32)
        for g in range(GROUP):
            h = kv * GROUP + g
            z = _dot_nt(_head_q(q, h, col), kc) - _slope(h) * cdist
            z = jnp.where(cvalid, z, NEG)
            e = jnp.exp(z - z.max(axis=-1, keepdims=True))
            p = jnp.where(cvalid, e / e.sum(axis=-1, keepdims=True), 0.0)
            outs.append(_dot(p.astype(BF16), vc))
            psum = psum + p
        imp = _split_dot(psum, cover_ref[...])
        jj = lax.broadcasted_iota(jnp.int32, (BLK, nsel), 1)
        tb = (n * BLK + lax.broadcasted_iota(jnp.int32, (BLK, nsel), 0)) // L_SEL
        forced = (jj == 0) | (jj == tb) | (jj == tb - 1)
        cur = jnp.where(forced, BIG, jnp.where(jj > tb, -BIG, imp))
        picked = jnp.zeros((BLK, nsel), jnp.bool_)
        for _ in range(min(N_SEL, nsel)):
            m = cur.max(axis=-1, keepdims=True)
            first = jnp.where(cur == m, jj, nsel).min(axis=-1, keepdims=True)
            hit = jj == first
            picked = picked | hit
            cur = jnp.where(hit, -BIG, cur)
        sel_ref[:, kv * nsel:(kv + 1) * nsel] = jnp.where(picked, 1.0, 0.0).astype(sel_ref.dtype)
    for j in range(N_HEADS // 2):
        o_ref[:, j * LANES_V7X:(j + 1) * LANES_V7X] = jnp.where(
            col < HEAD_DIM, outs[2 * j], outs[2 * j + 1]).astype(o_ref.dtype)


def _cmpsel(p_all, kcmp, vcmp, cover):
    s = p_all.shape[0]
    nc, nsel = cover.shape
    return pl.pallas_call(
        _cmpsel_kernel,
        grid=(s // BLK,),
        in_specs=[pl.BlockSpec((BLK, MIX_W), lambda n: (n, P_QB // MIX_W)),
                  pl.BlockSpec((nc, 2 * LANES_V7X), lambda n: (0, 0)),
                  pl.BlockSpec((nc, 2 * LANES_V7X), lambda n: (0, 0)),
                  pl.BlockSpec((nc, nsel), lambda n: (0, 0))],
        out_specs=[pl.BlockSpec((BLK, MIX_W), lambda n: (n, 0)),
                   pl.BlockSpec((BLK, N_KV * nsel), lambda n: (n, 0))],
        out_shape=[jax.ShapeDtypeStruct((s, MIX_W), BF16),
                   jax.ShapeDtypeStruct((s, N_KV * nsel), BF16)],
        compiler_params=_params(("parallel",), 48),
        name="nsa_cmp_select",
    )(p_all, kcmp, vcmp, cover)


def _selattn_kernel(q_ref, sel_ref, k_ref, v_ref, o_ref, q8_ref, m_ref, l_ref, r_ref):
    n = pl.program_id(0)
    nsel = sel_ref.shape[1] // N_KV
    grp = GROUP * BLK
    row = lax.broadcasted_iota(jnp.int32, (BLK, BLK), 0)
    col = lax.broadcasted_iota(jnp.int32, (BLK, BLK), 1)
    q = q_ref[...]
    for h in range(N_HEADS):
        q8_ref[h * BLK:(h + 1) * BLK, :] = _head_q(q, h, col)
    m_ref[...] = jnp.full_like(m_ref, NEG)
    l_ref[...] = jnp.zeros_like(l_ref)
    r_ref[...] = jnp.zeros_like(r_ref)
    ej = lax.broadcasted_iota(jnp.int32, (nsel, BLK), 0)
    ek = lax.broadcasted_iota(jnp.int32, (nsel, BLK), 1) // L_SEL

    def body(c, carry):
        k0 = pl.multiple_of(c * BLK, BLK)
        dist = (n - c) * BLK + row - col
        expand = jnp.where(ej == c * (BLK // L_SEL) + ek, 1.0, 0.0).astype(BF16)
        for kv in range(N_KV):
            chosen = _dot(sel_ref[:, kv * nsel:(kv + 1) * nsel], expand)
            valid1 = (chosen > 0.5) & (dist >= 0)

            @pl.when(jnp.max(jnp.where(valid1, 1.0, 0.0)) > 0.5)
            def _():
                sl = slice(kv * grp, (kv + 1) * grp)
                valid = jnp.concatenate([valid1] * GROUP, axis=0)
                bias = jnp.concatenate(
                    [_slope(kv * GROUP + g) * dist.astype(F32) for g in range(GROUP)], axis=0)
                z = _dot_nt(q8_ref[sl, :], k_ref[pl.ds(k0, BLK), kv * LANES_V7X:(kv + 1) * LANES_V7X])
                z = jnp.where(valid, z - bias, NEG)
                m_old = m_ref[sl, :]
                m_new = jnp.maximum(m_old, z.max(axis=-1, keepdims=True))
                alpha = jnp.exp(m_old - m_new)
                e = jnp.where(valid, jnp.exp(z - m_new), 0.0)
                l_ref[sl, :] = alpha * l_ref[sl, :] + e.sum(axis=-1, keepdims=True)
                r_ref[sl, :] = alpha * r_ref[sl, :] + _dot(
                    e.astype(BF16), v_ref[pl.ds(k0, BLK), kv * LANES_V7X:(kv + 1) * LANES_V7X])
                m_ref[sl, :] = m_new
        return carry

    lax.fori_loop(0, n + 1, body, 0)
    out = r_ref[...] / l_ref[...]
    for j in range(N_HEADS // 2):
        o_ref[:, j * LANES_V7X:(j + 1) * LANES_V7X] = jnp.where(
            col < HEAD_DIM, out[2 * j * BLK:(2 * j + 1) * BLK, :],
            out[(2 * j + 1) * BLK:(2 * j + 2) * BLK, :]).astype(o_ref.dtype)


def _selattn(p_all, sel):
    s = p_all.shape[0]
    kw = 2 * LANES_V7X
    return pl.pallas_call(
        _selattn_kernel,
        grid=(s // BLK,),
        in_specs=[pl.BlockSpec((BLK, MIX_W), lambda n: (n, P_QB // MIX_W)),
                  pl.BlockSpec((BLK, sel.shape[1]), lambda n: (n, 0)),
                  _resident((s, kw), lambda n: (0, P_KS // kw)),
                  _resident((s, kw), lambda n: (0, P_VS // kw))],
        out_specs=pl.BlockSpec((BLK, MIX_W), lambda n: (n, 0)),
        out_shape=jax.ShapeDtypeStruct((s, MIX_W), BF16),
        scratch_shapes=[pltpu.VMEM((N_HEADS * BLK, LANES_V7X), BF16),
                        pltpu.VMEM((N_HEADS * BLK, 1), F32),
                        pltpu.VMEM((N_HEADS * BLK, 1), F32),
                        pltpu.VMEM((N_HEADS * BLK, LANES_V7X), F32)],
        compiler_params=_params(("parallel",), 48),
        name="nsa_selected",
    )(p_all, sel, p_all, p_all)


def _merge_kernel(x_ref, g1_ref, g2_ref, oa_ref, ocmp_ref, oslc_ref, owin_ref, oc_ref, gb_ref,
                  gexp_ref, wgm_ref, wbr_ref, wout_ref, xo_ref, hn_ref):
    x = x_ref[...]
    hn = _rms(x, g1_ref[...]).astype(BF16)
    gb = jax.nn.sigmoid(gb_ref[...].astype(F32))
    gexp = _split_dot(gb, gexp_ref[...])
    ob = (gexp[:, 0:MIX_W] * ocmp_ref[...].astype(F32)
          + gexp[:, MIX_W:2 * MIX_W] * oslc_ref[...].astype(F32)
          + gexp[:, 2 * MIX_W:3 * MIX_W] * owin_ref[...].astype(F32)).astype(BF16)
    mixed = jnp.zeros(x.shape, F32)
    for i, o in enumerate((oa_ref[...], ob, oc_ref[...])):
        gm = jax.nn.sigmoid(_dot(hn, wgm_ref[:, i * D_MODEL:(i + 1) * D_MODEL]))
        mixed = mixed + gm * _dot(o, wbr_ref[i])
    xn = x + _dot(mixed.astype(BF16), wout_ref[...])
    xo_ref[...] = xn
    hn_ref[...] = _rms(xn, g2_ref[...]).astype(BF16)


def _merge(x, g1, g2, oa, ocmp, oslc, owin, oc, p_all, gexp, wgm, wbr, wout, tm=256):
    s, d = x.shape
    row = lambda w: pl.BlockSpec((tm, w), lambda i: (i, 0))
    return pl.pallas_call(
        _merge_kernel,
        grid=(s // tm,),
        in_specs=[row(d),
                  pl.BlockSpec((1, d), lambda i: (0, 0)),
                  pl.BlockSpec((1, d), lambda i: (0, 0)),
                  row(MIX_W), row(MIX_W), row(MIX_W), row(MIX_W), row(MIX_W),
                  pl.BlockSpec((tm, LANES_V7X), lambda i: (i, P_GB // LANES_V7X)),
                  _resident(gexp.shape, lambda i: (0, 0)),
                  _resident(wgm.shape, lambda i: (0, 0)),
                  _resident(wbr.shape, lambda i: (0, 0, 0)),
                  _resident(wout.shape, lambda i: (0, 0))],
        out_specs=[row(d), row(d)],
        out_shape=[jax.ShapeDtypeStruct((s, d), F32), jax.ShapeDtypeStruct((s, d), BF16)],
        compiler_params=_params(("parallel",), 48),
        name="merge",
    )(x, g1, g2, oa, ocmp, oslc, owin, oc, p_all, gexp, wgm, wbr, wout)


RANKS = PEER_TOPK + 1
RANK_ROWS = 24


def _top_values(xs, out_refs):
    rows = [lax.broadcasted_iota(jnp.int32, x.shape, 0) for x in xs]
    for o in out_refs:
        o[...] = jnp.full(o.shape, -BIG, F32)

    def body(r, curs):
        nxt = []
        for cur, rw, o in zip(curs, rows, out_refs):
            m = cur.max(axis=0, keepdims=True)
            o[pl.ds(r, 1), :] = m
            first = jnp.where(cur == m, rw, cur.shape[0]).min(axis=0, keepdims=True)
            nxt.append(jnp.where(rw == first, -BIG, cur))
        return tuple(nxt)

    lax.fori_loop(0, RANKS, body, tuple(xs))


def _route_kernel(xn_ref, wq_ref, keys_ref, thr_ref, e1_ref, e2_ref,
                  qr_ref, v1_ref, v2_ref, sc_ref):
    qr_ref[...] = _dot_nt(wq_ref[...], xn_ref[...]).astype(BF16)
    for h in range(PEER_HEADS):
        base = h * 2 * PK_DIM
        s1 = _dot(keys_ref[0, h], qr_ref[base:base + PK_DIM, :])
        s2 = _dot(keys_ref[1, h], qr_ref[base + PK_DIM:base + 2 * PK_DIM, :])
        _top_values((s1, s2), (v1_ref, v2_ref))
        cand = jnp.concatenate(
            [v1_ref[0:1, :] + v2_ref[...]]
            + [v1_ref[a:a + 1, :] + v2_ref[0:8, :] for a in range(1, 8)]
            + [v1_ref[8:RANK_ROWS, :] + v2_ref[0:1, :]], axis=0)
        _top_values((cand,), (sc_ref,))
        sc = sc_ref[0:PEER_TOPK, :]
        z = jnp.exp(sc - sc[0:1, :]).sum(axis=0, keepdims=True)
        tau = 0.5 * (sc_ref[PEER_TOPK - 1:PEER_TOPK, :] + sc_ref[PEER_TOPK:RANKS, :])
        m1 = v1_ref[0:1, :]
        m2 = v2_ref[0:1, :]
        thr_ref[h] = jnp.exp(tau - m2 - s1)
        e1_ref[h] = jnp.exp(s1 - m1) / z
        e2_ref[h] = jnp.exp(s2 - m2)


def _route(xn, wq_t, keys, tm=256):
    s, d = xn.shape
    big = lambda: pl.BlockSpec((PEER_HEADS, N_KEYS, tm), lambda i: (0, 0, i))
    big_shape = jax.ShapeDtypeStruct((PEER_HEADS, N_KEYS, s), F32)
    return pl.pallas_call(
        _route_kernel,
        grid=(s // tm,),
        in_specs=[pl.BlockSpec((tm, d), lambda i: (i, 0)),
                  _resident(wq_t.shape, lambda i: (0, 0)),
                  _resident(keys.shape, lambda i: (0, 0, 0, 0))],
        out_specs=[big(), big(), big()],
        out_shape=[big_shape, big_shape, big_shape],
        scratch_shapes=[pltpu.VMEM((PEER_HEADS * 2 * PK_DIM, tm), BF16),
                        pltpu.VMEM((RANK_ROWS, tm), F32),
                        pltpu.VMEM((RANK_ROWS, tm), F32),
                        pltpu.VMEM((RANK_ROWS, tm), F32)],
        compiler_params=_params(("parallel",), 48),
        name="peer_route",
    )(xn, wq_t, keys)


def _experts_kernel(x_ref, xn_ref, u_ref, vt_ref, thr_ref, e1_ref, e2_ref,
                    o_ref, acc_ref, *, chunks):
    j = pl.program_id(1)

    @pl.when(j == 0)
    def _():
        acc_ref[...] = jnp.zeros_like(acc_ref)

    hid = _gelu(_dot_nt(u_ref[...], xn_ref[...]))
    parts = []
    for c in range(chunks):
        i1 = j * chunks + c
        w = jnp.zeros((N_KEYS, hid.shape[1]), F32)
        for h in range(PEER_HEADS):
            e2 = e2_ref[h]
            w = w + e1_ref[h, pl.ds(i1, 1), :] * jnp.where(e2 >= thr_ref[h, pl.ds(i1, 1), :], e2, 0.0)
        parts.append((w * hid[c * N_KEYS:(c + 1) * N_KEYS, :]).astype(BF16))
    acc_ref[...] += _dot(vt_ref[...], jnp.concatenate(parts, axis=0))

    @pl.when(j == pl.num_programs(1) - 1)
    def _():
        o_ref[...] = x_ref[...] + acc_ref[...].T


def _experts(x, xn, u, vt, thr, e1, e2, tm=512, te=1024):
    s, d = x.shape
    big = lambda: pl.BlockSpec((PEER_HEADS, N_KEYS, tm), lambda i, j: (0, 0, i))
    return pl.pallas_call(
        functools.partial(_experts_kernel, chunks=te // N_KEYS),
        grid=(s // tm, N_EXPERTS // te),
        in_specs=[pl.BlockSpec((tm, d), lambda i, j: (i, 0)),
                  pl.BlockSpec((tm, d), lambda i, j: (i, 0)),
                  pl.BlockSpec((te, d), lambda i, j: (j, 0)),
                  pl.BlockSpec((d, te), lambda i, j: (0, j)),
                  big(), big(), big()],
        out_specs=pl.BlockSpec((tm, d), lambda i, j: (i, 0)),
        out_shape=jax.ShapeDtypeStruct((s, d), F32),
        scratch_shapes=[pltpu.VMEM((d, tm), F32)],
        compiler_params=_params(("parallel", "arbitrary"), 56),
        name="peer_experts",
    )(x, xn, u, vt, thr, e1, e2)


def _norm_kernel(x_ref, g_ref, o_ref):
    o_ref[...] = _rms(x_ref[...], g_ref[...])


def _final_norm(x, g, tm=1024):
    s, d = x.shape
    return pl.pallas_call(
        _norm_kernel,
        grid=(s // tm,),
        in_specs=[pl.BlockSpec((tm, d), lambda i: (i, 0)), pl.BlockSpec((1, d), lambda i: (0, 0))],
        out_specs=pl.BlockSpec((tm, d), lambda i: (i, 0)),
        out_shape=jax.ShapeDtypeStruct((s, d), F32),
        compiler_params=_params(("parallel",), 32),
        name="final_norm",
    )(x, g)


def _dup_heads(w):
    a, b = w[:, :HEAD_DIM], w[:, HEAD_DIM:]
    return jnp.concatenate([a, a, b, b], axis=1)


def _pack_w_in(w_in):
    parts, start = [], 0
    for w in IN_WIDTHS:
        parts.append(w_in[:, start:start + w])
        start += w
    qa, ka, va, qb, kcb, vcb, ksb, vsb, kwb, vwb, gb, qc, kc, vc, gm = parts
    scale = HEAD_DIM ** -0.5
    d = w_in.shape[0]
    packed = jnp.concatenate(
        [qc * scale, kc, vc, qa * scale, qb * scale,
         _dup_heads(ka), _dup_heads(va), _dup_heads(ksb), _dup_heads(vsb),
         _dup_heads(kwb), _dup_heads(vwb), kcb, vcb,
         gb, jnp.zeros((d, P_COLS - P_GB - gb.shape[1]), F32)], axis=1)
    return packed.astype(BF16), gm.astype(BF16)


def _gate_expand():
    r = jnp.arange(LANES_V7X)[:, None]
    c = jnp.arange(3 * MIX_W)[None, :]
    return ((r == 3 * ((c % MIX_W) // HEAD_DIM) + c // MIX_W) & (r < 3 * N_HEADS)).astype(BF16)


def _cover(nc, nsel):
    c_start = jnp.arange(nc)[:, None] * D_CMP
    j_start = jnp.arange(nsel)[None, :] * L_SEL
    real = jnp.arange(nc)[:, None] < nc - 1
    return (real & (c_start < j_start + L_SEL) & (c_start + L_CMP - 1 >= j_start)).astype(BF16)


def _layer(x, norm1, w_in, sinks, cmp_pos, cmp_w1, cmp_w2, w_branch, w_out,
           norm2, peer_wq, peer_keys, peer_u, peer_v):
    s = x.shape[0]
    nc = s // D_CMP
    w_pack, w_gm = _pack_w_in(w_in)
    p_all = _normproj(x, norm1.reshape(1, -1), w_pack)

    o_a = _banded(p_all, P_QA, P_KA, P_VA, A_WINDOW, sinks)
    o_c = _stick(p_all)

    y_k = p_all[:, P_KCB:P_KCB + LANES_V7X].reshape(nc, D_CMP * LANES_V7X)
    y_v = p_all[:, P_VCB:P_VCB + LANES_V7X].reshape(nc, D_CMP * LANES_V7X)
    k_cmp = _compress(y_k, cmp_w1[0], cmp_pos[0], cmp_w2[0])
    v_cmp = _compress(y_v, cmp_w1[1], cmp_pos[1], cmp_w2[1])
    o_cmp, sel = _cmpsel(p_all, k_cmp, v_cmp, _cover(nc, s // L_SEL))
    o_slc = _selattn(p_all, sel)
    o_win = _banded(p_all, P_QB, P_KW, P_VW, B_WINDOW)

    x, xn = _merge(x, norm1.reshape(1, -1), norm2.reshape(1, -1), o_a, o_cmp, o_slc, o_win, o_c,
                   p_all, _gate_expand(), w_gm, w_branch.astype(BF16), w_out.astype(BF16))

    wq_t = peer_wq.reshape(D_MODEL, PEER_HEADS * 2 * PK_DIM).T.astype(BF16)
    thr, e1, e2 = _route(xn, wq_t, peer_keys.astype(BF16))
    return _experts(x, xn, peer_u.astype(BF16), peer_v.T.astype(BF16), thr, e1, e2)


def kernel(x, norm1, w_in, sinks, cmp_pos, cmp_w1, cmp_w2, w_branch, w_out, norm2, peer_wq,
           peer_keys, peer_u, peer_v, norm_f):
    b, s, d = x.shape
    assert b == 1 and d == D_MODEL and s % (8 * BLK) == 0
    h = x.reshape(s, d)
    for l in range(norm1.shape[0]):
        h = _layer(h, norm1[l], w_in[l], sinks[l], cmp_pos[l], cmp_w1[l], cmp_w2[l],
                   w_branch[l], w_out[l], norm2[l], peer_wq[l], peer_keys[l],
                   peer_u[l], peer_v[l])
    return _final_norm(h, norm_f.reshape(1, -1)).reshape(b, s, d)
```

```python
import functools

import jax
import jax.numpy as jnp
from jax import lax
from jax.experimental import pallas as pl
from jax.experimental.pallas import tpu as pltpu

F32 = jnp.float32
BF16 = jnp.bfloat16

LANES_V7X = 128
VMEM_BYTES_V7X = 64 * 1024 * 1024

D_MODEL = 1024
HEAD_DIM = 64
BLK = 128
N_HEADS = 8
N_KV = 2
GROUP = N_HEADS // N_KV
MIX_W = N_HEADS * HEAD_DIM
A_WINDOW = 128
B_WINDOW = 512
L_CMP = 32
D_CMP = 16
CMP_HID = 128
L_SEL = 64
N_SEL = 16
PEER_HEADS = 8
N_KEYS = 128
N_EXPERTS = N_KEYS * N_KEYS
PK_DIM = 128
PEER_TOPK = 16
RMS_EPS = 1e-6
NEG = -1e30
BIG = 3.0e38
STICK_DONE = 110.0

IN_WIDTHS = (MIX_W, 128, 128, MIX_W, 128, 128, 128, 128, 128, 128, N_HEADS * 3,
             MIX_W, MIX_W, MIX_W, 3 * D_MODEL)

P_QC, P_KC, P_VC, P_QA, P_QB = 0, 512, 1024, 1536, 2048
P_KA, P_VA, P_KS, P_VS, P_KW, P_VW = 2560, 2816, 3072, 3328, 3584, 3840
P_KCB, P_VCB, P_GB = 4096, 4224, 4352
P_COLS = 4608


def _slope(h):
    return 2.0 ** (-(h + 1))


def _params(dims, vmem_mb):
    return pltpu.CompilerParams(dimension_semantics=dims,
                                vmem_limit_bytes=vmem_mb * 1024 * 1024)


def _resident(shape, index_map):
    return pl.BlockSpec(shape, index_map, pipeline_mode=pl.Buffered(1))


def _rms(x, g):
    ms = jnp.mean(x * x, axis=-1, keepdims=True)
    return x * lax.rsqrt(ms + RMS_EPS) * g


def _dot(a, b):
    return jnp.dot(a, b, preferred_element_type=F32)


def _dot_nt(a, b):
    return lax.dot_general(a, b, (((1,), (1,)), ((), ())), preferred_element_type=F32)


def _split_dot(x, w):
    hi = x.astype(BF16)
    lo = (x - hi.astype(F32)).astype(BF16)
    return _dot(hi, w) + _dot(lo, w)


def _gelu(x):
    return 0.5 * x * (1.0 + jnp.tanh(0.7978845608028654 * (x + 0.044715 * (x * x * x))))


def _head_q(q, h, lane):
    chunk = q[:, (h // 2) * LANES_V7X:(h // 2 + 1) * LANES_V7X]
    keep = (lane < HEAD_DIM) if h % 2 == 0 else (lane >= HEAD_DIM)
    return jnp.where(keep, chunk, jnp.zeros_like(chunk))


def _normproj_kernel(x_ref, g_ref, w_ref, o_ref, hn_ref):
    @pl.when(pl.program_id(1) == 0)
    def _():
        hn_ref[...] = _rms(x_ref[...], g_ref[...]).astype(BF16)

    o_ref[...] = _dot(hn_ref[...], w_ref[...]).astype(o_ref.dtype)


def _normproj(x, g, w, tm=1024, tn=1536):
    s, d = x.shape
    n = w.shape[1]
    return pl.pallas_call(
        _normproj_kernel,
        grid=(s // tm, n // tn),
        in_specs=[pl.BlockSpec((tm, d), lambda i, j: (i, 0)),
                  pl.BlockSpec((1, d), lambda i, j: (0, 0)),
                  pl.BlockSpec((d, tn), lambda i, j: (0, j))],
        out_specs=pl.BlockSpec((tm, tn), lambda i, j: (i, j)),
        out_shape=jax.ShapeDtypeStruct((s, n), BF16),
        scratch_shapes=[pltpu.VMEM((tm, d), BF16)],
        compiler_params=_params(("parallel", "arbitrary"), 48),
        name="normproj",
    )(x, g, w)


def _banded_kernel(*refs, nprev, use_sink):
    q_ref = refs[0]
    k_refs = refs[1:2 + nprev]
    v_refs = refs[2 + nprev:3 + 2 * nprev]
    pos = 3 + 2 * nprev
    if use_sink:
        sink_ref = refs[pos]
        pos += 1
    o_ref = refs[pos]

    n = pl.program_id(0)
    window = nprev * BLK
    row = lax.broadcasted_iota(jnp.int32, (BLK, BLK), 0)
    col = lax.broadcasted_iota(jnp.int32, (BLK, BLK), 1)
    q = q_ref[...]
    outs = []
    for kv in range(N_KV):
        heads = range(kv * GROUP, (kv + 1) * GROUP)
        lanes = slice(kv * LANES_V7X, (kv + 1) * LANES_V7X)
        q4 = jnp.concatenate([_head_q(q, h, col) for h in heads], axis=0)
        logits = []
        for p in range(nprev + 1):
            dist = row - col + (nprev - p) * BLK
            s_pos = (n - nprev + p) * BLK + col
            valid = (s_pos >= 0) & (dist >= 0) & (dist < window)
            distf = dist.astype(F32)
            bias = jnp.concatenate([jnp.where(valid, _slope(h) * distf, -NEG) for h in heads], axis=0)
            logits.append(jnp.maximum(_dot_nt(q4, k_refs[p][:, lanes]) - bias, NEG))
        m = logits[0].max(axis=-1, keepdims=True)
        for lg in logits[1:]:
            m = jnp.maximum(m, lg.max(axis=-1, keepdims=True))
        if use_sink:
            sink = jnp.concatenate(
                [jnp.broadcast_to(sink_ref[:, h:h + 1], (BLK, 1)) for h in heads], axis=0)
            m = jnp.maximum(m, sink)
            den = jnp.exp(sink - m)
        else:
            den = jnp.zeros_like(m)
        acc = jnp.zeros((GROUP * BLK, LANES_V7X), F32)
        for p in range(nprev + 1):
            e = jnp.exp(logits[p] - m)
            den = den + e.sum(axis=-1, keepdims=True)
            acc = acc + _dot(e.astype(BF16), v_refs[p][:, lanes])
        out = acc / den
        outs += [out[g * BLK:(g + 1) * BLK, :] for g in range(GROUP)]
    for j in range(N_HEADS // 2):
        o_ref[:, j * LANES_V7X:(j + 1) * LANES_V7X] = jnp.where(
            col < HEAD_DIM, outs[2 * j], outs[2 * j + 1]).astype(o_ref.dtype)


def _banded(p_all, q_off, k_off, v_off, window, sinks=None):
    s = p_all.shape[0]
    nprev = window // BLK
    kw = 2 * LANES_V7X

    def kv_spec(off, p):
        return pl.BlockSpec((BLK, kw), lambda n: (jnp.maximum(n - nprev + p, 0), off // kw))

    in_specs = [pl.BlockSpec((BLK, MIX_W), lambda n: (n, q_off // MIX_W))]
    in_specs += [kv_spec(k_off, p) for p in range(nprev + 1)]
    in_specs += [kv_spec(v_off, p) for p in range(nprev + 1)]
    args = [p_all] * (3 + 2 * nprev)
    if sinks is not None:
        in_specs.append(pl.BlockSpec((1, N_HEADS), lambda n: (0, 0)))
        args.append(sinks.reshape(1, N_HEADS))
    return pl.pallas_call(
        functools.partial(_banded_kernel, nprev=nprev, use_sink=sinks is not None),
        grid=(s // BLK,),
        in_specs=in_specs,
        out_specs=pl.BlockSpec((BLK, MIX_W), lambda n: (n, 0)),
        out_shape=jax.ShapeDtypeStruct((s, MIX_W), BF16),
        compiler_params=_params(("parallel",), 32),
        name="banded_w%d" % window,
    )(*args)


def _stick_kernel(q_ref, k_ref, v_ref, o_ref, q8_ref, acc_ref, r_ref):
    n = pl.program_id(0)
    rows = N_HEADS * BLK
    pair = 2 * BLK
    row = lax.broadcasted_iota(jnp.int32, (BLK, BLK), 0)
    col = lax.broadcasted_iota(jnp.int32, (BLK, BLK), 1)
    suffix_ones = jnp.concatenate(
        [(row > col).astype(BF16), jnp.ones((BLK, BLK), BF16)], axis=1)
    q = q_ref[...]
    for h in range(N_HEADS):
        q8_ref[h * BLK:(h + 1) * BLK, :] = _head_q(q, h, col)
    acc_ref[...] = jnp.zeros_like(acc_ref)
    r_ref[...] = jnp.zeros_like(r_ref)

    def tile(kb, diag):
        k0 = pl.multiple_of(kb * BLK, BLK)
        z = jnp.concatenate(
            [_dot_nt(q8_ref[j * pair:(j + 1) * pair, :],
                     k_ref[pl.ds(k0, BLK), j * LANES_V7X:(j + 1) * LANES_V7X])
             for j in range(N_HEADS // 2)], axis=0)
        sp = jnp.maximum(z, 0.0) + jnp.log(1.0 + jnp.exp(-jnp.abs(z)))
        if diag:
            t_in = lax.broadcasted_iota(jnp.int32, (rows, BLK), 0) & (BLK - 1)
            valid = lax.broadcasted_iota(jnp.int32, (rows, BLK), 1) < t_in
            spm = jnp.where(valid, sp, 0.0)
        else:
            spm = sp
        ct = _split_dot(spm, suffix_ones)
        a = jnp.exp(z - sp - ct[:, :BLK] - acc_ref[...])
        if diag:
            a = jnp.where(valid, a, 0.0)
        a = a.astype(BF16)
        for j in range(N_HEADS // 2):
            r_ref[j * pair:(j + 1) * pair, :] += _dot(
                a[j * pair:(j + 1) * pair, :],
                v_ref[pl.ds(k0, BLK), j * LANES_V7X:(j + 1) * LANES_V7X])
        acc_ref[...] += ct[:, BLK:]
        return jnp.min(acc_ref[...])

    def cond(carry):
        i, acc_min = carry
        return (i <= n) & (acc_min < STICK_DONE)

    def body(carry):
        i, _ = carry
        return i + 1, tile(n - i, False)

    lax.while_loop(cond, body, (jnp.int32(1), tile(n, True)))
    for j in range(N_HEADS // 2):
        o_ref[:, j * LANES_V7X:(j + 1) * LANES_V7X] = jnp.where(
            col < HEAD_DIM, r_ref[2 * j * BLK:(2 * j + 1) * BLK, :],
            r_ref[(2 * j + 1) * BLK:(2 * j + 2) * BLK, :]).astype(o_ref.dtype)


def _stick(p_all):
    s = p_all.shape[0]
    return pl.pallas_call(
        _stick_kernel,
        grid=(s // BLK,),
        in_specs=[pl.BlockSpec((BLK, MIX_W), lambda n: (n, P_QC // MIX_W)),
                  _resident((s, MIX_W), lambda n: (0, P_KC // MIX_W)),
                  _resident((s, MIX_W), lambda n: (0, P_VC // MIX_W))],
        out_specs=pl.BlockSpec((BLK, MIX_W), lambda n: (n, 0)),
        out_shape=jax.ShapeDtypeStruct((s, MIX_W), BF16),
        scratch_shapes=[pltpu.VMEM((N_HEADS * BLK, LANES_V7X), BF16),
                        pltpu.VMEM((N_HEADS * BLK, BLK), F32),
                        pltpu.VMEM((N_HEADS * BLK, LANES_V7X), F32)],
        compiler_params=_params(("parallel",), 48),
        name="stick_breaking",
    )(p_all, p_all, p_all)


def _compress_kernel(y_ref, w1_ref, pos_ref, w1f_ref, w2_ref, o_ref):
    nc = y_ref.shape[0]
    ab = _dot(y_ref[...], w1_ref[...])
    bias = _dot(pos_ref[...], w1f_ref[...])[0:1, :]
    for hh in range(N_KV):
        top = ab[:, hh * CMP_HID:(hh + 1) * CMP_HID]
        bot = ab[:, (N_KV + hh) * CMP_HID:(N_KV + hh + 1) * CMP_HID]
        hid = top + pltpu.roll(bot, nc - 1, 0) + bias
        o_ref[:, hh * LANES_V7X:(hh + 1) * LANES_V7X] = _dot(
            _gelu(hid).astype(BF16), w2_ref[...]).astype(o_ref.dtype)


def _compress(y, w1, pos, w2):
    nc = y.shape[0]
    half = L_CMP // 2
    zeros = jnp.zeros((half, HEAD_DIM, CMP_HID), F32)

    def place(w, hh):
        parts = [w, zeros] if hh == 0 else [zeros, w]
        return jnp.concatenate(parts, axis=1).reshape(half * 2 * HEAD_DIM, CMP_HID)

    w1cat = jnp.concatenate([place(w1[:half], 0), place(w1[:half], 1),
                             place(w1[half:], 0), place(w1[half:], 1)], axis=1).astype(BF16)
    posf = jnp.broadcast_to(pos.reshape(1, L_CMP * HEAD_DIM), (8, L_CMP * HEAD_DIM)).astype(BF16)
    w1f = w1.reshape(L_CMP * HEAD_DIM, CMP_HID).astype(BF16)
    w2d = jnp.concatenate([w2, w2], axis=1).astype(BF16)
    return pl.pallas_call(
        _compress_kernel,
        out_shape=jax.ShapeDtypeStruct((nc, 2 * LANES_V7X), BF16),
        compiler_params=pltpu.CompilerParams(vmem_limit_bytes=48 * 1024 * 1024),
        name="nsa_compress",
    )(y, w1cat, posf, w1f, w2d)


def _cmpsel_kernel(q_ref, kc_ref, vc_ref, cover_ref, o_ref, sel_ref, any_ref):
    n = pl.program_id(0)
    nc = kc_ref.shape[0]
    nsel = cover_ref.shape[1]
    col = lax.broadcasted_iota(jnp.int32, (BLK, LANES_V7X), 1)
    t = n * BLK + lax.broadcasted_iota(jnp.int32, (BLK, nc), 0)
    c_end = lax.broadcasted_iota(jnp.int32, (BLK, nc), 1) * D_CMP + (L_CMP - 1)
    cvalid = c_end <= t
    cdist = (t - c_end).astype(F32)
    q = q_ref[...]
    outs = []
    for kv in range(N_KV):
        kc = kc_ref[:, kv * LANES_V7X:(kv + 1) * LANES_V7X]
        vc = vc_ref[:, kv * LANES_V7X:(kv + 1) * LANES_V7X]
        psum = jnp.zeros((BLK, nc), F32)
        for g in range(GROUP):
            h = kv * GROUP + g
            z = _dot_nt(_head_q(q, h, col), kc) - _slope(h) * cdist
            z = jnp.where(cvalid, z, NEG)
            e = jnp.exp(z - z.max(axis=-1, keepdims=True))
            p = jnp.where(cvalid, e / e.sum(axis=-1, keepdims=True), 0.0)
            outs.append(_dot(p.astype(BF16), vc))
            psum = psum + p
        imp = _split_dot(psum, cover_ref[...])
        jj = lax.broadcasted_iota(jnp.int32, (BLK, nsel), 1)
        tb = (n * BLK + lax.broadcasted_iota(jnp.int32, (BLK, nsel), 0)) // L_SEL
        forced = (jj == 0) | (jj == tb) | (jj == tb - 1)
        cur = jnp.where(forced, BIG, jnp.where(jj > tb, -BIG, imp))
        picked = jnp.zeros((BLK, nsel), jnp.bool_)
        for _ in range(min(N_SEL, nsel)):
            m = cur.max(axis=-1, keepdims=True)
            first = jnp.where(cur == m, jj, nsel).min(axis=-1, keepdims=True)
            hit = jj == first
            picked = picked | hit
            cur = jnp.where(hit, -BIG, cur)
        chosen = jnp.where(picked, 1.0, 0.0)
        sel_ref[:, kv * nsel:(kv + 1) * nsel] = chosen.astype(sel_ref.dtype)
        any_ref[0, :, kv * nsel:(kv + 1) * nsel] = chosen.max(axis=0, keepdims=True).astype(jnp.int32)
    for j in range(N_HEADS // 2):
        o_ref[:, j * LANES_V7X:(j + 1) * LANES_V7X] = jnp.where(
            col < HEAD_DIM, outs[2 * j], outs[2 * j + 1]).astype(o_ref.dtype)


def _cmpsel(p_all, kcmp, vcmp, cover):
    s = p_all.shape[0]
    nc, nsel = cover.shape
    return pl.pallas_call(
        _cmpsel_kernel,
        grid=(s // BLK,),
        in_specs=[pl.BlockSpec((BLK, MIX_W), lambda n: (n, P_QB // MIX_W)),
                  pl.BlockSpec((nc, 2 * LANES_V7X), lambda n: (0, 0)),
                  pl.BlockSpec((nc, 2 * LANES_V7X), lambda n: (0, 0)),
                  pl.BlockSpec((nc, nsel), lambda n: (0, 0))],
        out_specs=[pl.BlockSpec((BLK, MIX_W), lambda n: (n, 0)),
                   pl.BlockSpec((BLK, N_KV * nsel), lambda n: (n, 0)),
                   pl.BlockSpec((1, 1, N_KV * nsel), lambda n: (n, 0, 0))],
        out_shape=[jax.ShapeDtypeStruct((s, MIX_W), BF16),
                   jax.ShapeDtypeStruct((s, N_KV * nsel), BF16),
                   jax.ShapeDtypeStruct((s // BLK, 1, N_KV * nsel), jnp.int32)],
        compiler_params=_params(("parallel",), 48),
        name="nsa_cmp_select",
    )(p_all, kcmp, vcmp, cover)


def _selattn_kernel(q_ref, sel_ref, any_ref, k_ref, v_ref, o_ref, q8_ref, m_ref, l_ref, r_ref):
    n = pl.program_id(0)
    nsel = sel_ref.shape[1] // N_KV
    grp = GROUP * BLK
    row = lax.broadcasted_iota(jnp.int32, (BLK, BLK), 0)
    col = lax.broadcasted_iota(jnp.int32, (BLK, BLK), 1)
    q = q_ref[...]
    for h in range(N_HEADS):
        q8_ref[h * BLK:(h + 1) * BLK, :] = _head_q(q, h, col)
    m_ref[...] = jnp.full_like(m_ref, NEG)
    l_ref[...] = jnp.zeros_like(l_ref)
    r_ref[...] = jnp.zeros_like(r_ref)
    ej = lax.broadcasted_iota(jnp.int32, (nsel, BLK), 0)
    ek = lax.broadcasted_iota(jnp.int32, (nsel, BLK), 1) // L_SEL

    def body(c, carry):
        per_chunk = BLK // L_SEL
        for kv in range(N_KV):
            hits = any_ref[0, 0, kv * nsel + per_chunk * c]
            for b in range(1, per_chunk):
                hits = hits + any_ref[0, 0, kv * nsel + per_chunk * c + b]

            @pl.when(hits > 0)
            def _():
                k0 = pl.multiple_of(c * BLK, BLK)
                dist = (n - c) * BLK + row - col
                expand = jnp.where(ej == c * per_chunk + ek, 1.0, 0.0).astype(BF16)
                chosen = _dot(sel_ref[:, kv * nsel:(kv + 1) * nsel], expand)
                valid1 = (chosen > 0.5) & (dist >= 0)
                sl = slice(kv * grp, (kv + 1) * grp)
                valid = jnp.concatenate([valid1] * GROUP, axis=0)
                bias = jnp.concatenate(
                    [_slope(kv * GROUP + g) * dist.astype(F32) for g in range(GROUP)], axis=0)
                z = _dot_nt(q8_ref[sl, :], k_ref[pl.ds(k0, BLK), kv * LANES_V7X:(kv + 1) * LANES_V7X])
                z = jnp.where(valid, z - bias, NEG)
                m_old = m_ref[sl, :]
                m_new = jnp.maximum(m_old, z.max(axis=-1, keepdims=True))
                alpha = jnp.exp(m_old - m_new)
                e = jnp.where(valid, jnp.exp(z - m_new), 0.0)
                l_ref[sl, :] = alpha * l_ref[sl, :] + e.sum(axis=-1, keepdims=True)
                r_ref[sl, :] = alpha * r_ref[sl, :] + _dot(
                    e.astype(BF16), v_ref[pl.ds(k0, BLK), kv * LANES_V7X:(kv + 1) * LANES_V7X])
                m_ref[sl, :] = m_new
        return carry

    lax.fori_loop(0, n + 1, body, 0)
    out = r_ref[...] / l_ref[...]
    for j in range(N_HEADS // 2):
        o_ref[:, j * LANES_V7X:(j + 1) * LANES_V7X] = jnp.where(
            col < HEAD_DIM, out[2 * j * BLK:(2 * j + 1) * BLK, :],
            out[(2 * j + 1) * BLK:(2 * j + 2) * BLK, :]).astype(o_ref.dtype)


def _selattn(p_all, sel, sel_any):
    s = p_all.shape[0]
    kw = 2 * LANES_V7X
    return pl.pallas_call(
        _selattn_kernel,
        grid=(s // BLK,),
        in_specs=[pl.BlockSpec((BLK, MIX_W), lambda n: (n, P_QB // MIX_W)),
                  pl.BlockSpec((BLK, sel.shape[1]), lambda n: (n, 0)),
                  pl.BlockSpec((1, 1, sel.shape[1]), lambda n: (n, 0, 0), memory_space=pltpu.SMEM),
                  _resident((s, kw), lambda n: (0, P_KS // kw)),
                  _resident((s, kw), lambda n: (0, P_VS // kw))],
        out_specs=pl.BlockSpec((BLK, MIX_W), lambda n: (n, 0)),
        out_shape=jax.ShapeDtypeStruct((s, MIX_W), BF16),
        scratch_shapes=[pltpu.VMEM((N_HEADS * BLK, LANES_V7X), BF16),
                        pltpu.VMEM((N_HEADS * BLK, 1), F32),
                        pltpu.VMEM((N_HEADS * BLK, 1), F32),
                        pltpu.VMEM((N_HEADS * BLK, LANES_V7X), F32)],
        compiler_params=_params(("parallel",), 48),
        name="nsa_selected",
    )(p_all, sel, sel_any, p_all, p_all)


def _merge_kernel(x_ref, g1_ref, g2_ref, oa_ref, ocmp_ref, oslc_ref, owin_ref, oc_ref, gb_ref,
                  gexp_ref, wgm_ref, wbr_ref, wout_ref, xo_ref, hn_ref):
    x = x_ref[...]
    hn = _rms(x, g1_ref[...]).astype(BF16)
    gb = jax.nn.sigmoid(gb_ref[...].astype(F32))
    gexp = _split_dot(gb, gexp_ref[...])
    ob = (gexp[:, 0:MIX_W] * ocmp_ref[...].astype(F32)
          + gexp[:, MIX_W:2 * MIX_W] * oslc_ref[...].astype(F32)
          + gexp[:, 2 * MIX_W:3 * MIX_W] * owin_ref[...].astype(F32)).astype(BF16)
    mixed = jnp.zeros(x.shape, F32)
    for i, o in enumerate((oa_ref[...], ob, oc_ref[...])):
        gm = jax.nn.sigmoid(_dot(hn, wgm_ref[:, i * D_MODEL:(i + 1) * D_MODEL]))
        mixed = mixed + gm * _dot(o, wbr_ref[i])
    xn = x + _dot(mixed.astype(BF16), wout_ref[...])
    xo_ref[...] = xn
    hn_ref[...] = _rms(xn, g2_ref[...]).astype(BF16)


def _merge(x, g1, g2, oa, ocmp, oslc, owin, oc, p_all, gexp, wgm, wbr, wout, tm=256):
    s, d = x.shape
    row = lambda w: pl.BlockSpec((tm, w), lambda i: (i, 0))
    return pl.pallas_call(
        _merge_kernel,
        grid=(s // tm,),
        in_specs=[row(d),
                  pl.BlockSpec((1, d), lambda i: (0, 0)),
                  pl.BlockSpec((1, d), lambda i: (0, 0)),
                  row(MIX_W), row(MIX_W), row(MIX_W), row(MIX_W), row(MIX_W),
                  pl.BlockSpec((tm, LANES_V7X), lambda i: (i, P_GB // LANES_V7X)),
                  _resident(gexp.shape, lambda i: (0, 0)),
                  _resident(wgm.shape, lambda i: (0, 0)),
                  _resident(wbr.shape, lambda i: (0, 0, 0)),
                  _resident(wout.shape, lambda i: (0, 0))],
        out_specs=[row(d), row(d)],
        out_shape=[jax.ShapeDtypeStruct((s, d), F32), jax.ShapeDtypeStruct((s, d), BF16)],
        compiler_params=_params(("parallel",), 48),
        name="merge",
    )(x, g1, g2, oa, ocmp, oslc, owin, oc, p_all, gexp, wgm, wbr, wout)


RANKS = PEER_TOPK + 1
RANK_ROWS = 24


def _top_values(xs, out_refs):
    rows = [lax.broadcasted_iota(jnp.int32, x.shape, 0) for x in xs]
    for o in out_refs:
        o[...] = jnp.full(o.shape, -BIG, F32)

    def body(r, curs):
        nxt = []
        for cur, rw, o in zip(curs, rows, out_refs):
            m = cur.max(axis=0, keepdims=True)
            o[pl.ds(r, 1), :] = m
            first = jnp.where(cur == m, rw, cur.shape[0]).min(axis=0, keepdims=True)
            nxt.append(jnp.where(rw == first, -BIG, cur))
        return tuple(nxt)

    lax.fori_loop(0, RANKS, body, tuple(xs))


def _route_kernel(xn_ref, wq_ref, keys_ref, thr_ref, e1_ref, e2_ref,
                  qr_ref, v1_ref, v2_ref, sc_ref):
    qr_ref[...] = _dot_nt(wq_ref[...], xn_ref[...]).astype(BF16)
    for h in range(PEER_HEADS):
        base = h * 2 * PK_DIM
        s1 = _dot(keys_ref[0, h], qr_ref[base:base + PK_DIM, :])
        s2 = _dot(keys_ref[1, h], qr_ref[base + PK_DIM:base + 2 * PK_DIM, :])
        _top_values((s1, s2), (v1_ref, v2_ref))
        cand = jnp.concatenate(
            [v1_ref[0:1, :] + v2_ref[...]]
            + [v1_ref[a:a + 1, :] + v2_ref[0:8, :] for a in range(1, 8)]
            + [v1_ref[8:RANK_ROWS, :] + v2_ref[0:1, :]], axis=0)
        _top_values((cand,), (sc_ref,))
        sc = sc_ref[0:PEER_TOPK, :]
        z = jnp.exp(sc - sc[0:1, :]).sum(axis=0, keepdims=True)
        tau = 0.5 * (sc_ref[PEER_TOPK - 1:PEER_TOPK, :] + sc_ref[PEER_TOPK:RANKS, :])
        m1 = v1_ref[0:1, :]
        m2 = v2_ref[0:1, :]
        thr_ref[h] = jnp.exp(tau - m2 - s1)
        e1_ref[h] = jnp.exp(s1 - m1) / z
        e2_ref[h] = jnp.exp(s2 - m2)


def _route(xn, wq_t, keys, tm=256):
    s, d = xn.shape
    big = lambda: pl.BlockSpec((PEER_HEADS, N_KEYS, tm), lambda i: (0, 0, i))
    big_shape = jax.ShapeDtypeStruct((PEER_HEADS, N_KEYS, s), F32)
    return pl.pallas_call(
        _route_kernel,
        grid=(s // tm,),
        in_specs=[pl.BlockSpec((tm, d), lambda i: (i, 0)),
                  _resident(wq_t.shape, lambda i: (0, 0)),
                  _resident(keys.shape, lambda i: (0, 0, 0, 0))],
        out_specs=[big(), big(), big()],
        out_shape=[big_shape, big_shape, big_shape],
        scratch_shapes=[pltpu.VMEM((PEER_HEADS * 2 * PK_DIM, tm), BF16),
                        pltpu.VMEM((RANK_ROWS, tm), F32),
                        pltpu.VMEM((RANK_ROWS, tm), F32),
                        pltpu.VMEM((RANK_ROWS, tm), F32)],
        compiler_params=_params(("parallel",), 48),
        name="peer_route",
    )(xn, wq_t, keys)


def _experts_kernel(x_ref, xn_ref, u_ref, vt_ref, thr_ref, e1_ref, e2_ref,
                    o_ref, acc_ref, *, chunks):
    j = pl.program_id(1)

    @pl.when(j == 0)
    def _():
        acc_ref[...] = jnp.zeros_like(acc_ref)

    hid = _gelu(_dot_nt(u_ref[...], xn_ref[...]))
    parts = []
    for c in range(chunks):
        i1 = j * chunks + c
        w = jnp.zeros((N_KEYS, hid.shape[1]), F32)
        for h in range(PEER_HEADS):
            e2 = e2_ref[h]
            w = w + e1_ref[h, pl.ds(i1, 1), :] * jnp.where(e2 >= thr_ref[h, pl.ds(i1, 1), :], e2, 0.0)
        parts.append((w * hid[c * N_KEYS:(c + 1) * N_KEYS, :]).astype(BF16))
    acc_ref[...] += _dot(vt_ref[...], jnp.concatenate(parts, axis=0))

    @pl.when(j == pl.num_programs(1) - 1)
    def _():
        o_ref[...] = x_ref[...] + acc_ref[...].T


def _experts(x, xn, u, vt, thr, e1, e2, tm=512, te=1024):
    s, d = x.shape
    big = lambda: pl.BlockSpec((PEER_HEADS, N_KEYS, tm), lambda i, j: (0, 0, i))
    return pl.pallas_call(
        functools.partial(_experts_kernel, chunks=te // N_KEYS),
        grid=(s // tm, N_EXPERTS // te),
        in_specs=[pl.BlockSpec((tm, d), lambda i, j: (i, 0)),
                  pl.BlockSpec((tm, d), lambda i, j: (i, 0)),
                  pl.BlockSpec((te, d), lambda i, j: (j, 0)),
                  pl.BlockSpec((d, te), lambda i, j: (0, j)),
                  big(), big(), big()],
        out_specs=pl.BlockSpec((tm, d), lambda i, j: (i, 0)),
        out_shape=jax.ShapeDtypeStruct((s, d), F32),
        scratch_shapes=[pltpu.VMEM((d, tm), F32)],
        compiler_params=_params(("parallel", "arbitrary"), 56),
        name="peer_experts",
    )(x, xn, u, vt, thr, e1, e2)


def _norm_kernel(x_ref, g_ref, o_ref):
    o_ref[...] = _rms(x_ref[...], g_ref[...])


def _final_norm(x, g, tm=1024):
    s, d = x.shape
    return pl.pallas_call(
        _norm_kernel,
        grid=(s // tm,),
        in_specs=[pl.BlockSpec((tm, d), lambda i: (i, 0)), pl.BlockSpec((1, d), lambda i: (0, 0))],
        out_specs=pl.BlockSpec((tm, d), lambda i: (i, 0)),
        out_shape=jax.ShapeDtypeStruct((s, d), F32),
        compiler_params=_params(("parallel",), 32),
        name="final_norm",
    )(x, g)


def _dup_heads(w):
    a, b = w[:, :HEAD_DIM], w[:, HEAD_DIM:]
    return jnp.concatenate([a, a, b, b], axis=1)


def _pack_w_in(w_in):
    parts, start = [], 0
    for w in IN_WIDTHS:
        parts.append(w_in[:, start:start + w])
        start += w
    qa, ka, va, qb, kcb, vcb, ksb, vsb, kwb, vwb, gb, qc, kc, vc, gm = parts
    scale = HEAD_DIM ** -0.5
    d = w_in.shape[0]
    packed = jnp.concatenate(
        [qc * scale, kc, vc, qa * scale, qb * scale,
         _dup_heads(ka), _dup_heads(va), _dup_heads(ksb), _dup_heads(vsb),
         _dup_heads(kwb), _dup_heads(vwb), kcb, vcb,
         gb, jnp.zeros((d, P_COLS - P_GB - gb.shape[1]), F32)], axis=1)
    return packed.astype(BF16), gm.astype(BF16)


def _gate_expand():
    r = jnp.arange(LANES_V7X)[:, None]
    c = jnp.arange(3 * MIX_W)[None, :]
    return ((r == 3 * ((c % MIX_W) // HEAD_DIM) + c // MIX_W) & (r < 3 * N_HEADS)).astype(BF16)


def _cover(nc, nsel):
    c_start = jnp.arange(nc)[:, None] * D_CMP
    j_start = jnp.arange(nsel)[None, :] * L_SEL
    real = jnp.arange(nc)[:, None] < nc - 1
    return (real & (c_start < j_start + L_SEL) & (c_start + L_CMP - 1 >= j_start)).astype(BF16)


def _layer(x, norm1, w_in, sinks, cmp_pos, cmp_w1, cmp_w2, w_branch, w_out,
           norm2, peer_wq, peer_keys, peer_u, peer_v):
    s = x.shape[0]
    nc = s // D_CMP
    w_pack, w_gm = _pack_w_in(w_in)
    p_all = _normproj(x, norm1.reshape(1, -1), w_pack)

    o_a = _banded(p_all, P_QA, P_KA, P_VA, A_WINDOW, sinks)
    o_c = _stick(p_all)

    y_k = p_all[:, P_KCB:P_KCB + LANES_V7X].reshape(nc, D_CMP * LANES_V7X)
    y_v = p_all[:, P_VCB:P_VCB + LANES_V7X].reshape(nc, D_CMP * LANES_V7X)
    k_cmp = _compress(y_k, cmp_w1[0], cmp_pos[0], cmp_w2[0])
    v_cmp = _compress(y_v, cmp_w1[1], cmp_pos[1], cmp_w2[1])
    o_cmp, sel, sel_any = _cmpsel(p_all, k_cmp, v_cmp, _cover(nc, s // L_SEL))
    o_slc = _selattn(p_all, sel, sel_any)
    o_win = _banded(p_all, P_QB, P_KW, P_VW, B_WINDOW)

    x, xn = _merge(x, norm1.reshape(1, -1), norm2.reshape(1, -1), o_a, o_cmp, o_slc, o_win, o_c,
                   p_all, _gate_expand(), w_gm, w_branch.astype(BF16), w_out.astype(BF16))

    wq_t = peer_wq.reshape(D_MODEL, PEER_HEADS * 2 * PK_DIM).T.astype(BF16)
    thr, e1, e2 = _route(xn, wq_t, peer_keys.astype(BF16))
    return _experts(x, xn, peer_u.astype(BF16), peer_v.T.astype(BF16), thr, e1, e2)


def kernel(x, norm1, w_in, sinks, cmp_pos, cmp_w1, cmp_w2, w_branch, w_out, norm2, peer_wq,
           peer_keys, peer_u, peer_v, norm_f):
    b, s, d = x.shape
    assert b == 1 and d == D_MODEL and s % (8 * BLK) == 0
    h = x.reshape(s, d)
    for l in range(norm1.shape[0]):
        h = _layer(h, norm1[l], w_in[l], sinks[l], cmp_pos[l], cmp_w1[l], cmp_w2[l],
                   w_branch[l], w_out[l], norm2[l], peer_wq[l], peer_keys[l],
                   peer_u[l], peer_v[l])
    return _final_norm(h, norm_f.reshape(1, -1)).reshape(b, s, d)
```

```python
import functools

import jax
import jax.numpy as jnp
from jax import lax
from jax.experimental import pallas as pl
from jax.experimental.pallas import tpu as pltpu

F32 = jnp.float32
BF16 = jnp.bfloat16

LANES_V7X = 128
VMEM_BYTES_V7X = 64 * 1024 * 1024

D_MODEL = 1024
HEAD_DIM = 64
BLK = 128
N_HEADS = 8
N_KV = 2
GROUP = N_HEADS // N_KV
MIX_W = N_HEADS * HEAD_DIM
A_WINDOW = 128
B_WINDOW = 512
L_CMP = 32
D_CMP = 16
CMP_HID = 128
L_SEL = 64
N_SEL = 16
PEER_HEADS = 8
N_KEYS = 128
N_EXPERTS = N_KEYS * N_KEYS
PK_DIM = 128
PEER_TOPK = 16
RMS_EPS = 1e-6
NEG = -1e30
BIG = 3.0e38
STICK_DONE = 110.0

IN_WIDTHS = (MIX_W, 128, 128, MIX_W, 128, 128, 128, 128, 128, 128, N_HEADS * 3,
             MIX_W, MIX_W, MIX_W, 3 * D_MODEL)

P_QC, P_KC, P_VC, P_QA, P_QB = 0, 512, 1024, 1536, 2048
P_KA, P_VA, P_KS, P_VS, P_KW, P_VW = 2560, 2816, 3072, 3328, 3584, 3840
P_KCB, P_VCB, P_GB = 4096, 4224, 4352
P_COLS = 4608


def _slope(h):
    return 2.0 ** (-(h + 1))


def _params(dims, vmem_mb, flags=None):
    return pltpu.CompilerParams(dimension_semantics=dims,
                                vmem_limit_bytes=vmem_mb * 1024 * 1024, flags=flags)


def _resident(shape, index_map):
    return pl.BlockSpec(shape, index_map, pipeline_mode=pl.Buffered(1))


def _rms(x, g):
    ms = jnp.mean(x * x, axis=-1, keepdims=True)
    return x * lax.rsqrt(ms + RMS_EPS) * g


def _dot(a, b):
    return jnp.dot(a, b, preferred_element_type=F32)


def _dot_nt(a, b):
    return lax.dot_general(a, b, (((1,), (1,)), ((), ())), preferred_element_type=F32)


def _split_dot(x, w):
    hi = x.astype(BF16)
    lo = (x - hi.astype(F32)).astype(BF16)
    return _dot(hi, w) + _dot(lo, w)


def _gelu(x):
    return 0.5 * x * (1.0 + jnp.tanh(0.7978845608028654 * (x + 0.044715 * (x * x * x))))


def _head_q(q, h, lane):
    chunk = q[:, (h // 2) * LANES_V7X:(h // 2 + 1) * LANES_V7X]
    keep = (lane < HEAD_DIM) if h % 2 == 0 else (lane >= HEAD_DIM)
    return jnp.where(keep, chunk, jnp.zeros_like(chunk))


def _normproj_kernel(x_ref, g_ref, w_ref, o_ref, hn_ref):
    @pl.when(pl.program_id(1) == 0)
    def _():
        hn_ref[...] = _rms(x_ref[...], g_ref[...]).astype(BF16)

    o_ref[...] = _dot(hn_ref[...], w_ref[...]).astype(o_ref.dtype)


def _normproj(x, g, w, tm=1024, tn=1536):
    s, d = x.shape
    n = w.shape[1]
    return pl.pallas_call(
        _normproj_kernel,
        grid=(s // tm, n // tn),
        in_specs=[pl.BlockSpec((tm, d), lambda i, j: (i, 0)),
                  pl.BlockSpec((1, d), lambda i, j: (0, 0)),
                  pl.BlockSpec((d, tn), lambda i, j: (0, j))],
        out_specs=pl.BlockSpec((tm, tn), lambda i, j: (i, j)),
        out_shape=jax.ShapeDtypeStruct((s, n), BF16),
        scratch_shapes=[pltpu.VMEM((tm, d), BF16)],
        compiler_params=_params(("parallel", "arbitrary"), 48),
        name="normproj",
    )(x, g, w)


def _banded_kernel(*refs, nprev, use_sink):
    q_ref = refs[0]
    k_refs = refs[1:2 + nprev]
    v_refs = refs[2 + nprev:3 + 2 * nprev]
    pos = 3 + 2 * nprev
    if use_sink:
        sink_ref = refs[pos]
        pos += 1
    o_ref = refs[pos]

    n = pl.program_id(0)
    window = nprev * BLK
    row = lax.broadcasted_iota(jnp.int32, (BLK, BLK), 0)
    col = lax.broadcasted_iota(jnp.int32, (BLK, BLK), 1)
    q = q_ref[...]
    outs = []
    for kv in range(N_KV):
        heads = range(kv * GROUP, (kv + 1) * GROUP)
        lanes = slice(kv * LANES_V7X, (kv + 1) * LANES_V7X)
        q4 = jnp.concatenate([_head_q(q, h, col) for h in heads], axis=0)
        logits = []
        for p in range(nprev + 1):
            dist = row - col + (nprev - p) * BLK
            s_pos = (n - nprev + p) * BLK + col
            valid = (s_pos >= 0) & (dist >= 0) & (dist < window)
            distf = dist.astype(F32)
            bias = jnp.concatenate([jnp.where(valid, _slope(h) * distf, -NEG) for h in heads], axis=0)
            logits.append(jnp.maximum(_dot_nt(q4, k_refs[p][:, lanes]) - bias, NEG))
        m = logits[0].max(axis=-1, keepdims=True)
        for lg in logits[1:]:
            m = jnp.maximum(m, lg.max(axis=-1, keepdims=True))
        if use_sink:
            sink = jnp.concatenate(
                [jnp.broadcast_to(sink_ref[:, h:h + 1], (BLK, 1)) for h in heads], axis=0)
            m = jnp.maximum(m, sink)
            den = jnp.exp(sink - m)
        else:
            den = jnp.zeros_like(m)
        acc = jnp.zeros((GROUP * BLK, LANES_V7X), F32)
        for p in range(nprev + 1):
            e = jnp.exp(logits[p] - m)
            den = den + e.sum(axis=-1, keepdims=True)
            acc = acc + _dot(e.astype(BF16), v_refs[p][:, lanes])
        out = acc / den
        outs += [out[g * BLK:(g + 1) * BLK, :] for g in range(GROUP)]
    for j in range(N_HEADS // 2):
        o_ref[:, j * LANES_V7X:(j + 1) * LANES_V7X] = jnp.where(
            col < HEAD_DIM, outs[2 * j], outs[2 * j + 1]).astype(o_ref.dtype)


def _banded(p_all, q_off, k_off, v_off, window, sinks=None):
    s = p_all.shape[0]
    nprev = window // BLK
    kw = 2 * LANES_V7X

    def kv_spec(off, p):
        return pl.BlockSpec((BLK, kw), lambda n: (jnp.maximum(n - nprev + p, 0), off // kw))

    in_specs = [pl.BlockSpec((BLK, MIX_W), lambda n: (n, q_off // MIX_W))]
    in_specs += [kv_spec(k_off, p) for p in range(nprev + 1)]
    in_specs += [kv_spec(v_off, p) for p in range(nprev + 1)]
    args = [p_all] * (3 + 2 * nprev)
    if sinks is not None:
        in_specs.append(pl.BlockSpec((1, N_HEADS), lambda n: (0, 0)))
        args.append(sinks.reshape(1, N_HEADS))
    return pl.pallas_call(
        functools.partial(_banded_kernel, nprev=nprev, use_sink=sinks is not None),
        grid=(s // BLK,),
        in_specs=in_specs,
        out_specs=pl.BlockSpec((BLK, MIX_W), lambda n: (n, 0)),
        out_shape=jax.ShapeDtypeStruct((s, MIX_W), BF16),
        compiler_params=_params(("parallel",), 32),
        name="banded_w%d" % window,
    )(*args)


def _stick_kernel(q_ref, k_ref, v_ref, o_ref, q8_ref, acc_ref, r_ref):
    n = pl.program_id(0)
    rows = N_HEADS * BLK
    pair = 2 * BLK
    row = lax.broadcasted_iota(jnp.int32, (BLK, BLK), 0)
    col = lax.broadcasted_iota(jnp.int32, (BLK, BLK), 1)
    suffix_ones = jnp.concatenate(
        [(row > col).astype(BF16), jnp.ones((BLK, BLK), BF16)], axis=1)
    q = q_ref[...]
    for h in range(N_HEADS):
        q8_ref[h * BLK:(h + 1) * BLK, :] = _head_q(q, h, col)
    acc_ref[...] = jnp.zeros_like(acc_ref)
    r_ref[...] = jnp.zeros_like(r_ref)

    def tile(kb, diag):
        k0 = pl.multiple_of(kb * BLK, BLK)
        z = jnp.concatenate(
            [_dot_nt(q8_ref[j * pair:(j + 1) * pair, :],
                     k_ref[pl.ds(k0, BLK), j * LANES_V7X:(j + 1) * LANES_V7X])
             for j in range(N_HEADS // 2)], axis=0)
        sp = jnp.maximum(z, 0.0) + jnp.log(1.0 + jnp.exp(-jnp.abs(z)))
        if diag:
            t_in = lax.broadcasted_iota(jnp.int32, (rows, BLK), 0) & (BLK - 1)
            valid = lax.broadcasted_iota(jnp.int32, (rows, BLK), 1) < t_in
            spm = jnp.where(valid, sp, 0.0)
        else:
            spm = sp
        ct = _split_dot(spm, suffix_ones)
        a = jnp.exp(z - sp - ct[:, :BLK] - acc_ref[...])
        if diag:
            a = jnp.where(valid, a, 0.0)
        a = a.astype(BF16)
        for j in range(N_HEADS // 2):
            r_ref[j * pair:(j + 1) * pair, :] += _dot(
                a[j * pair:(j + 1) * pair, :],
                v_ref[pl.ds(k0, BLK), j * LANES_V7X:(j + 1) * LANES_V7X])
        acc_ref[...] += ct[:, BLK:]
        return jnp.min(acc_ref[...])

    def cond(carry):
        i, acc_min = carry
        return (i <= n) & (acc_min < STICK_DONE)

    def body(carry):
        i, _ = carry
        return i + 1, tile(n - i, False)

    lax.while_loop(cond, body, (jnp.int32(1), tile(n, True)))
    for j in range(N_HEADS // 2):
        o_ref[:, j * LANES_V7X:(j + 1) * LANES_V7X] = jnp.where(
            col < HEAD_DIM, r_ref[2 * j * BLK:(2 * j + 1) * BLK, :],
            r_ref[(2 * j + 1) * BLK:(2 * j + 2) * BLK, :]).astype(o_ref.dtype)


def _stick(p_all):
    s = p_all.shape[0]
    return pl.pallas_call(
        _stick_kernel,
        grid=(s // BLK,),
        in_specs=[pl.BlockSpec((BLK, MIX_W), lambda n: (n, P_QC // MIX_W)),
                  _resident((s, MIX_W), lambda n: (0, P_KC // MIX_W)),
                  _resident((s, MIX_W), lambda n: (0, P_VC // MIX_W))],
        out_specs=pl.BlockSpec((BLK, MIX_W), lambda n: (n, 0)),
        out_shape=jax.ShapeDtypeStruct((s, MIX_W), BF16),
        scratch_shapes=[pltpu.VMEM((N_HEADS * BLK, LANES_V7X), BF16),
                        pltpu.VMEM((N_HEADS * BLK, BLK), F32),
                        pltpu.VMEM((N_HEADS * BLK, LANES_V7X), F32)],
        compiler_params=_params(("parallel",), 48),
        name="stick_breaking",
    )(p_all, p_all, p_all)


def _compress_kernel(y_ref, w1_ref, pos_ref, w1f_ref, w2_ref, o_ref):
    nc = y_ref.shape[0]
    ab = _dot(y_ref[...], w1_ref[...])
    bias = _dot(pos_ref[...], w1f_ref[...])[0:1, :]
    for hh in range(N_KV):
        top = ab[:, hh * CMP_HID:(hh + 1) * CMP_HID]
        bot = ab[:, (N_KV + hh) * CMP_HID:(N_KV + hh + 1) * CMP_HID]
        hid = top + pltpu.roll(bot, nc - 1, 0) + bias
        o_ref[:, hh * LANES_V7X:(hh + 1) * LANES_V7X] = _dot(
            _gelu(hid).astype(BF16), w2_ref[...]).astype(o_ref.dtype)


def _compress(y, w1, pos, w2):
    nc = y.shape[0]
    half = L_CMP // 2
    zeros = jnp.zeros((half, HEAD_DIM, CMP_HID), F32)

    def place(w, hh):
        parts = [w, zeros] if hh == 0 else [zeros, w]
        return jnp.concatenate(parts, axis=1).reshape(half * 2 * HEAD_DIM, CMP_HID)

    w1cat = jnp.concatenate([place(w1[:half], 0), place(w1[:half], 1),
                             place(w1[half:], 0), place(w1[half:], 1)], axis=1).astype(BF16)
    posf = jnp.broadcast_to(pos.reshape(1, L_CMP * HEAD_DIM), (8, L_CMP * HEAD_DIM)).astype(BF16)
    w1f = w1.reshape(L_CMP * HEAD_DIM, CMP_HID).astype(BF16)
    w2d = jnp.concatenate([w2, w2], axis=1).astype(BF16)
    return pl.pallas_call(
        _compress_kernel,
        out_shape=jax.ShapeDtypeStruct((nc, 2 * LANES_V7X), BF16),
        compiler_params=pltpu.CompilerParams(vmem_limit_bytes=48 * 1024 * 1024),
        name="nsa_compress",
    )(y, w1cat, posf, w1f, w2d)


def _cmpsel_kernel(q_ref, kc_ref, vc_ref, cover_ref, o_ref, sel_ref, any_ref):
    n = pl.program_id(0)
    nc = kc_ref.shape[0]
    nsel = cover_ref.shape[1]
    col = lax.broadcasted_iota(jnp.int32, (BLK, LANES_V7X), 1)
    t = n * BLK + lax.broadcasted_iota(jnp.int32, (BLK, nc), 0)
    c_end = lax.broadcasted_iota(jnp.int32, (BLK, nc), 1) * D_CMP + (L_CMP - 1)
    cvalid = c_end <= t
    cdist = (t - c_end).astype(F32)
    q = q_ref[...]
    outs = []
    for kv in range(N_KV):
        kc = kc_ref[:, kv * LANES_V7X:(kv + 1) * LANES_V7X]
        vc = vc_ref[:, kv * LANES_V7X:(kv + 1) * LANES_V7X]
        psum = jnp.zeros((BLK, nc), F32)
        for g in range(GROUP):
            h = kv * GROUP + g
            z = _dot_nt(_head_q(q, h, col), kc) - _slope(h) * cdist
            z = jnp.where(cvalid, z, NEG)
            e = jnp.exp(z - z.max(axis=-1, keepdims=True))
            p = jnp.where(cvalid, e / e.sum(axis=-1, keepdims=True), 0.0)
            outs.append(_dot(p.astype(BF16), vc))
            psum = psum + p
        imp = _split_dot(psum, cover_ref[...])
        jj = lax.broadcasted_iota(jnp.int32, (BLK, nsel), 1)
        tb = (n * BLK + lax.broadcasted_iota(jnp.int32, (BLK, nsel), 0)) // L_SEL
        forced = (jj == 0) | (jj == tb) | (jj == tb - 1)
        cur = jnp.where(forced, BIG, jnp.where(jj > tb, -BIG, imp))
        picked = jnp.zeros((BLK, nsel), jnp.bool_)
        for _ in range(min(N_SEL, nsel)):
            m = cur.max(axis=-1, keepdims=True)
            first = jnp.where(cur == m, jj, nsel).min(axis=-1, keepdims=True)
            hit = jj == first
            picked = picked | hit
            cur = jnp.where(hit, -BIG, cur)
        chosen = jnp.where(picked, 1.0, 0.0)
        sel_ref[:, kv * nsel:(kv + 1) * nsel] = chosen.astype(sel_ref.dtype)
        any_ref[0, :, kv * nsel:(kv + 1) * nsel] = chosen.max(axis=0, keepdims=True).astype(jnp.int32)
    for j in range(N_HEADS // 2):
        o_ref[:, j * LANES_V7X:(j + 1) * LANES_V7X] = jnp.where(
            col < HEAD_DIM, outs[2 * j], outs[2 * j + 1]).astype(o_ref.dtype)


def _cmpsel(p_all, kcmp, vcmp, cover):
    s = p_all.shape[0]
    nc, nsel = cover.shape
    return pl.pallas_call(
        _cmpsel_kernel,
        grid=(s // BLK,),
        in_specs=[pl.BlockSpec((BLK, MIX_W), lambda n: (n, P_QB // MIX_W)),
                  pl.BlockSpec((nc, 2 * LANES_V7X), lambda n: (0, 0)),
                  pl.BlockSpec((nc, 2 * LANES_V7X), lambda n: (0, 0)),
                  pl.BlockSpec((nc, nsel), lambda n: (0, 0))],
        out_specs=[pl.BlockSpec((BLK, MIX_W), lambda n: (n, 0)),
                   pl.BlockSpec((BLK, N_KV * nsel), lambda n: (n, 0)),
                   pl.BlockSpec((1, 1, N_KV * nsel), lambda n: (n, 0, 0))],
        out_shape=[jax.ShapeDtypeStruct((s, MIX_W), BF16),
                   jax.ShapeDtypeStruct((s, N_KV * nsel), BF16),
                   jax.ShapeDtypeStruct((s // BLK, 1, N_KV * nsel), jnp.int32)],
        compiler_params=_params(("parallel",), 48),
        name="nsa_cmp_select",
    )(p_all, kcmp, vcmp, cover)


def _selattn_kernel(q_ref, sel_ref, any_ref, k_ref, v_ref, o_ref, q8_ref, m_ref, l_ref, r_ref):
    n = pl.program_id(0)
    nsel = sel_ref.shape[1] // N_KV
    grp = GROUP * BLK
    row = lax.broadcasted_iota(jnp.int32, (BLK, BLK), 0)
    col = lax.broadcasted_iota(jnp.int32, (BLK, BLK), 1)
    q = q_ref[...]
    for h in range(N_HEADS):
        q8_ref[h * BLK:(h + 1) * BLK, :] = _head_q(q, h, col)
    m_ref[...] = jnp.full_like(m_ref, NEG)
    l_ref[...] = jnp.zeros_like(l_ref)
    r_ref[...] = jnp.zeros_like(r_ref)
    ej = lax.broadcasted_iota(jnp.int32, (nsel, BLK), 0)
    ek = lax.broadcasted_iota(jnp.int32, (nsel, BLK), 1) // L_SEL

    def body(c, carry):
        per_chunk = BLK // L_SEL
        for kv in range(N_KV):
            hits = any_ref[0, 0, kv * nsel + per_chunk * c]
            for b in range(1, per_chunk):
                hits = hits + any_ref[0, 0, kv * nsel + per_chunk * c + b]

            @pl.when(hits > 0)
            def _():
                k0 = pl.multiple_of(c * BLK, BLK)
                dist = (n - c) * BLK + row - col
                expand = jnp.where(ej == c * per_chunk + ek, 1.0, 0.0).astype(BF16)
                chosen = _dot(sel_ref[:, kv * nsel:(kv + 1) * nsel], expand)
                valid1 = (chosen > 0.5) & (dist >= 0)
                sl = slice(kv * grp, (kv + 1) * grp)
                valid = jnp.concatenate([valid1] * GROUP, axis=0)
                bias = jnp.concatenate(
                    [_slope(kv * GROUP + g) * dist.astype(F32) for g in range(GROUP)], axis=0)
                z = _dot_nt(q8_ref[sl, :], k_ref[pl.ds(k0, BLK), kv * LANES_V7X:(kv + 1) * LANES_V7X])
                z = jnp.where(valid, z - bias, NEG)
                m_old = m_ref[sl, :]
                m_new = jnp.maximum(m_old, z.max(axis=-1, keepdims=True))
                alpha = jnp.exp(m_old - m_new)
                e = jnp.where(valid, jnp.exp(z - m_new), 0.0)
                l_ref[sl, :] = alpha * l_ref[sl, :] + e.sum(axis=-1, keepdims=True)
                r_ref[sl, :] = alpha * r_ref[sl, :] + _dot(
                    e.astype(BF16), v_ref[pl.ds(k0, BLK), kv * LANES_V7X:(kv + 1) * LANES_V7X])
                m_ref[sl, :] = m_new
        return carry

    lax.fori_loop(0, n + 1, body, 0)
    out = r_ref[...] / l_ref[...]
    for j in range(N_HEADS // 2):
        o_ref[:, j * LANES_V7X:(j + 1) * LANES_V7X] = jnp.where(
            col < HEAD_DIM, out[2 * j * BLK:(2 * j + 1) * BLK, :],
            out[(2 * j + 1) * BLK:(2 * j + 2) * BLK, :]).astype(o_ref.dtype)


def _selattn(p_all, sel, sel_any):
    s = p_all.shape[0]
    kw = 2 * LANES_V7X
    return pl.pallas_call(
        _selattn_kernel,
        grid=(s // BLK,),
        in_specs=[pl.BlockSpec((BLK, MIX_W), lambda n: (n, P_QB // MIX_W)),
                  pl.BlockSpec((BLK, sel.shape[1]), lambda n: (n, 0)),
                  pl.BlockSpec((1, 1, sel.shape[1]), lambda n: (n, 0, 0), memory_space=pltpu.SMEM),
                  _resident((s, kw), lambda n: (0, P_KS // kw)),
                  _resident((s, kw), lambda n: (0, P_VS // kw))],
        out_specs=pl.BlockSpec((BLK, MIX_W), lambda n: (n, 0)),
        out_shape=jax.ShapeDtypeStruct((s, MIX_W), BF16),
        scratch_shapes=[pltpu.VMEM((N_HEADS * BLK, LANES_V7X), BF16),
                        pltpu.VMEM((N_HEADS * BLK, 1), F32),
                        pltpu.VMEM((N_HEADS * BLK, 1), F32),
                        pltpu.VMEM((N_HEADS * BLK, LANES_V7X), F32)],
        compiler_params=_params(("parallel",), 48),
        name="nsa_selected",
    )(p_all, sel, sel_any, p_all, p_all)


def _merge_kernel(x_ref, g1_ref, g2_ref, oa_ref, ocmp_ref, oslc_ref, owin_ref, oc_ref, gb_ref,
                  gexp_ref, wgm_ref, wbr_ref, wout_ref, xo_ref, hnt_ref):
    x = x_ref[...]
    hn = _rms(x, g1_ref[...]).astype(BF16)
    gb = jax.nn.sigmoid(gb_ref[...].astype(F32))
    gexp = _split_dot(gb, gexp_ref[...])
    ob = (gexp[:, 0:MIX_W] * ocmp_ref[...].astype(F32)
          + gexp[:, MIX_W:2 * MIX_W] * oslc_ref[...].astype(F32)
          + gexp[:, 2 * MIX_W:3 * MIX_W] * owin_ref[...].astype(F32)).astype(BF16)
    mixed = jnp.zeros(x.shape, F32)
    for i, o in enumerate((oa_ref[...], ob, oc_ref[...])):
        gm = jax.nn.sigmoid(_dot(hn, wgm_ref[:, i * D_MODEL:(i + 1) * D_MODEL]))
        mixed = mixed + gm * _dot(o, wbr_ref[i])
    xn = x + _dot(mixed.astype(BF16), wout_ref[...])
    xo_ref[...] = xn
    hnt_ref[...] = _rms(xn, g2_ref[...]).T.astype(BF16)


def _merge(x, g1, g2, oa, ocmp, oslc, owin, oc, p_all, gexp, wgm, wbr, wout, tm=256):
    s, d = x.shape
    row = lambda w: pl.BlockSpec((tm, w), lambda i: (i, 0))
    return pl.pallas_call(
        _merge_kernel,
        grid=(s // tm,),
        in_specs=[row(d),
                  pl.BlockSpec((1, d), lambda i: (0, 0)),
                  pl.BlockSpec((1, d), lambda i: (0, 0)),
                  row(MIX_W), row(MIX_W), row(MIX_W), row(MIX_W), row(MIX_W),
                  pl.BlockSpec((tm, LANES_V7X), lambda i: (i, P_GB // LANES_V7X)),
                  _resident(gexp.shape, lambda i: (0, 0)),
                  _resident(wgm.shape, lambda i: (0, 0)),
                  _resident(wbr.shape, lambda i: (0, 0, 0)),
                  _resident(wout.shape, lambda i: (0, 0))],
        out_specs=[row(d), pl.BlockSpec((d, tm), lambda i: (0, i))],
        out_shape=[jax.ShapeDtypeStruct((s, d), F32), jax.ShapeDtypeStruct((d, s), BF16)],
        compiler_params=_params(("parallel",), 48),
        name="merge",
    )(x, g1, g2, oa, ocmp, oslc, owin, oc, p_all, gexp, wgm, wbr, wout)


SUBLANES_V7X = 8


def _exchange(seq, i, l):
    hi, lo = jnp.maximum(seq[i], seq[l]), jnp.minimum(seq[i], seq[l])
    seq[i], seq[l] = hi, lo


def _bitonic_merge(seq):
    n = len(seq)
    j = n // 2
    while j >= 1:
        for i in range(n):
            if i ^ j > i:
                _exchange(seq, i, i ^ j)
        j //= 2


def _top_values(tiles):
    seq = list(tiles)
    n = len(seq)
    k = 2
    while k <= n:
        j = k // 2
        while j >= 1:
            for i in range(n):
                l = i ^ j
                if l > i:
                    if i & k == 0:
                        _exchange(seq, i, l)
                    else:
                        _exchange(seq, l, i)
            j //= 2
        k *= 2
    lost = jnp.full(seq[0].shape, -BIG, F32)
    shift = SUBLANES_V7X // 2
    while shift >= 1:
        other = [pltpu.roll(s, shift, 0) for s in seq]
        lost = jnp.maximum(lost, pltpu.roll(lost, shift, 0))
        merged = []
        for r in range(n):
            merged.append(jnp.maximum(seq[r], other[n - 1 - r]))
            lost = jnp.maximum(lost, jnp.minimum(seq[r], other[n - 1 - r]))
        _bitonic_merge(merged)
        seq = merged
        shift //= 2
    return seq + [lost]


def _route_kernel(xnt_ref, wq_ref, keys_ref, thr_ref, e1_ref, e2_ref, qr_ref):
    qr_ref[...] = _dot(wq_ref[...], xnt_ref[...]).astype(BF16)
    tm = qr_ref.shape[1]
    sub = lax.broadcasted_iota(jnp.int32, (SUBLANES_V7X, tm), 0)
    fill = jnp.full((SUBLANES_V7X, tm), -BIG, F32)

    def tiles_of(x):
        return [x[r * SUBLANES_V7X:(r + 1) * SUBLANES_V7X, :] for r in range(x.shape[0] // SUBLANES_V7X)]

    def spread(vals):
        out = vals[0]
        for r in range(1, len(vals)):
            out = jnp.where(sub == r, vals[r], out)
        return out

    for h in range(PEER_HEADS):
        base = h * 2 * PK_DIM
        s1 = _dot(keys_ref[0, h], qr_ref[base:base + PK_DIM, :])
        s2 = _dot(keys_ref[1, h], qr_ref[base + PK_DIM:base + 2 * PK_DIM, :])
        v1 = _top_values(tiles_of(s1))
        v2 = _top_values(tiles_of(s2))
        v2_lo, v2_hi, v1_hi = spread(v2[0:8]), spread(v2[8:16]), spread(v1[8:16])
        cand = [v1[0] + v2_lo, v1[0] + v2_hi]
        cand += [v1[a] + v2_lo for a in range(1, 8)]
        cand += [v1_hi + v2[0], spread([v1[0] + v2[16], v1[16] + v2[0]] + [fill] * 6)]
        cand += [fill] * (PEER_TOPK - len(cand))
        sc = _top_values(cand)
        z = sc[0] - sc[0]
        for r in range(PEER_TOPK):
            z = z + jnp.exp(sc[r] - sc[0])
        tau = 0.5 * (sc[PEER_TOPK - 1] + sc[PEER_TOPK])[0:1, :]
        m1 = v1[0][0:1, :]
        m2 = v2[0][0:1, :]
        thr_ref[h] = jnp.exp(tau - m2 - s1)
        e1_ref[h] = jnp.exp(s1 - m1) / z[0:1, :]
        e2_ref[h] = jnp.exp(s2 - m2)


def _route(xnt, wq_t, keys, tm=256):
    d, s = xnt.shape
    big = lambda: pl.BlockSpec((PEER_HEADS, N_KEYS, tm), lambda i: (0, 0, i))
    big_shape = jax.ShapeDtypeStruct((PEER_HEADS, N_KEYS, s), F32)
    return pl.pallas_call(
        _route_kernel,
        grid=(s // tm,),
        in_specs=[pl.BlockSpec((d, tm), lambda i: (0, i)),
                  _resident(wq_t.shape, lambda i: (0, 0)),
                  _resident(keys.shape, lambda i: (0, 0, 0, 0))],
        out_specs=[big(), big(), big()],
        out_shape=[big_shape, big_shape, big_shape],
        scratch_shapes=[pltpu.VMEM((PEER_HEADS * 2 * PK_DIM, tm), BF16)],
        compiler_params=_params(("parallel",), 48),
        name="peer_route",
    )(xnt, wq_t, keys)


def _experts_kernel(x_ref, xnt_ref, u_ref, vt_ref, thr_ref, e1_ref, e2_ref,
                    o_ref, acc_ref, g_ref, w_ref, *, chunks):
    j = pl.program_id(1)

    @pl.when(j == 0)
    def _():
        acc_ref[...] = jnp.zeros_like(acc_ref)

    xnt = xnt_ref[...]
    for c in range(chunks):
        rows = slice(c * N_KEYS, (c + 1) * N_KEYS)
        g_ref[rows, :] = _dot(u_ref[rows, :], xnt)
        i1 = j * chunks + c
        w = None
        for h in range(PEER_HEADS):
            e2 = e2_ref[h]
            t = e1_ref[h, pl.ds(i1, 1), :] * jnp.where(e2 >= thr_ref[h, pl.ds(i1, 1), :], e2, 0.0)
            w = t if w is None else w + t
        w_ref[rows, :] = w
    acc_ref[...] += _dot(vt_ref[...], (_gelu(g_ref[...]) * w_ref[...]).astype(BF16))

    @pl.when(j == pl.num_programs(1) - 1)
    def _():
        o_ref[...] = x_ref[...] + acc_ref[...].T


def _experts(x, xnt, u, vt, thr, e1, e2, tm=512, te=1024):
    s, d = x.shape
    big = lambda: pl.BlockSpec((PEER_HEADS, N_KEYS, tm), lambda i, j: (0, 0, i))
    return pl.pallas_call(
        functools.partial(_experts_kernel, chunks=te // N_KEYS),
        grid=(s // tm, N_EXPERTS // te),
        in_specs=[pl.BlockSpec((tm, d), lambda i, j: (i, 0)),
                  pl.BlockSpec((d, tm), lambda i, j: (0, i)),
                  pl.BlockSpec((te, d), lambda i, j: (j, 0)),
                  pl.BlockSpec((d, te), lambda i, j: (0, j)),
                  big(), big(), big()],
        out_specs=pl.BlockSpec((tm, d), lambda i, j: (i, 0)),
        out_shape=jax.ShapeDtypeStruct((s, d), F32),
        scratch_shapes=[pltpu.VMEM((d, tm), F32), pltpu.VMEM((te, tm), F32), pltpu.VMEM((te, tm), F32)],
        compiler_params=_params(("parallel", "arbitrary"), 56),
        name="peer_experts",
    )(x, xnt, u, vt, thr, e1, e2)


def _norm_kernel(x_ref, g_ref, o_ref):
    o_ref[...] = _rms(x_ref[...], g_ref[...])


def _final_norm(x, g, tm=1024):
    s, d = x.shape
    return pl.pallas_call(
        _norm_kernel,
        grid=(s // tm,),
        in_specs=[pl.BlockSpec((tm, d), lambda i: (i, 0)), pl.BlockSpec((1, d), lambda i: (0, 0))],
        out_specs=pl.BlockSpec((tm, d), lambda i: (i, 0)),
        out_shape=jax.ShapeDtypeStruct((s, d), F32),
        compiler_params=_params(("parallel",), 32),
        name="final_norm",
    )(x, g)


def _dup_heads(w):
    a, b = w[:, :HEAD_DIM], w[:, HEAD_DIM:]
    return jnp.concatenate([a, a, b, b], axis=1)


def _pack_w_in(w_in):
    parts, start = [], 0
    for w in IN_WIDTHS:
        parts.append(w_in[:, start:start + w])
        start += w
    qa, ka, va, qb, kcb, vcb, ksb, vsb, kwb, vwb, gb, qc, kc, vc, gm = parts
    scale = HEAD_DIM ** -0.5
    d = w_in.shape[0]
    packed = jnp.concatenate(
        [qc * scale, kc, vc, qa * scale, qb * scale,
         _dup_heads(ka), _dup_heads(va), _dup_heads(ksb), _dup_heads(vsb),
         _dup_heads(kwb), _dup_heads(vwb), kcb, vcb,
         gb, jnp.zeros((d, P_COLS - P_GB - gb.shape[1]), F32)], axis=1)
    return packed.astype(BF16), gm.astype(BF16)


def _gate_expand():
    r = jnp.arange(LANES_V7X)[:, None]
    c = jnp.arange(3 * MIX_W)[None, :]
    return ((r == 3 * ((c % MIX_W) // HEAD_DIM) + c // MIX_W) & (r < 3 * N_HEADS)).astype(BF16)


def _cover(nc, nsel):
    c_start = jnp.arange(nc)[:, None] * D_CMP
    j_start = jnp.arange(nsel)[None, :] * L_SEL
    real = jnp.arange(nc)[:, None] < nc - 1
    return (real & (c_start < j_start + L_SEL) & (c_start + L_CMP - 1 >= j_start)).astype(BF16)


def _layer(x, norm1, w_in, sinks, cmp_pos, cmp_w1, cmp_w2, w_branch, w_out,
           norm2, peer_wq, peer_keys, peer_u, peer_v):
    s = x.shape[0]
    nc = s // D_CMP
    w_pack, w_gm = _pack_w_in(w_in)
    p_all = _normproj(x, norm1.reshape(1, -1), w_pack)

    o_a = _banded(p_all, P_QA, P_KA, P_VA, A_WINDOW, sinks)
    o_c = _stick(p_all)

    y_k = p_all[:, P_KCB:P_KCB + LANES_V7X].reshape(nc, D_CMP * LANES_V7X)
    y_v = p_all[:, P_VCB:P_VCB + LANES_V7X].reshape(nc, D_CMP * LANES_V7X)
    k_cmp = _compress(y_k, cmp_w1[0], cmp_pos[0], cmp_w2[0])
    v_cmp = _compress(y_v, cmp_w1[1], cmp_pos[1], cmp_w2[1])
    o_cmp, sel, sel_any = _cmpsel(p_all, k_cmp, v_cmp, _cover(nc, s // L_SEL))
    o_slc = _selattn(p_all, sel, sel_any)
    o_win = _banded(p_all, P_QB, P_KW, P_VW, B_WINDOW)

    x, xnt = _merge(x, norm1.reshape(1, -1), norm2.reshape(1, -1), o_a, o_cmp, o_slc, o_win, o_c,
                   p_all, _gate_expand(), w_gm, w_branch.astype(BF16), w_out.astype(BF16))

    wq_t = peer_wq.reshape(D_MODEL, PEER_HEADS * 2 * PK_DIM).T.astype(BF16)
    thr, e1, e2 = _route(xnt, wq_t, peer_keys.astype(BF16))
    return _experts(x, xnt, peer_u.astype(BF16), peer_v.T.astype(BF16), thr, e1, e2)


def kernel(x, norm1, w_in, sinks, cmp_pos, cmp_w1, cmp_w2, w_branch, w_out, norm2, peer_wq,
           peer_keys, peer_u, peer_v, norm_f):
    b, s, d = x.shape
    assert b == 1 and d == D_MODEL and s % (8 * BLK) == 0
    h = x.reshape(s, d)
    for l in range(norm1.shape[0]):
        h = _layer(h, norm1[l], w_in[l], sinks[l], cmp_pos[l], cmp_w1[l], cmp_w2[l],
                   w_branch[l], w_out[l], norm2[l], peer_wq[l], peer_keys[l],
                   peer_u[l], peer_v[l])
    return _final_norm(h, norm_f.reshape(1, -1)).reshape(b, s, d)
```

```python
import functools

import jax
import jax.numpy as jnp
from jax import lax
from jax.experimental import pallas as pl
from jax.experimental.pallas import tpu as pltpu

F32 = jnp.float32
BF16 = jnp.bfloat16

LANES_V7X = 128

D_MODEL = 1024
HEAD_DIM = 64
BLK = 128
N_HEADS = 8
N_KV = 2
GROUP = N_HEADS // N_KV
MIX_W = N_HEADS * HEAD_DIM
A_WINDOW = 128
B_WINDOW = 512
L_CMP = 32
D_CMP = 16
CMP_HID = 128
L_SEL = 64
N_SEL = 16
PEER_HEADS = 8
N_KEYS = 128
N_EXPERTS = N_KEYS * N_KEYS
PK_DIM = 128
PEER_TOPK = 16
RMS_EPS = 1e-6
NEG = -1e30
BIG = 3.0e38
STICK_DONE = 110.0

IN_WIDTHS = (MIX_W, 128, 128, MIX_W, 128, 128, 128, 128, 128, 128, N_HEADS * 3,
             MIX_W, MIX_W, MIX_W, 3 * D_MODEL)

P_QC, P_KC, P_VC, P_QA, P_QB = 0, 512, 1024, 1536, 2048
P_KA, P_VA, P_KS, P_VS, P_KW, P_VW = 2560, 2816, 3072, 3328, 3584, 3840
P_KCB, P_VCB, P_GB = 4096, 4224, 4352
P_COLS = 4608


def _slope(h):
    return 2.0 ** (-(h + 1))


def _params(dims, vmem_mb, flags=None):
    return pltpu.CompilerParams(dimension_semantics=dims,
                                vmem_limit_bytes=vmem_mb * 1024 * 1024, flags=flags)


def _resident(shape, index_map):
    return pl.BlockSpec(shape, index_map, pipeline_mode=pl.Buffered(1))


def _rms(x, g):
    ms = jnp.mean(x * x, axis=-1, keepdims=True)
    return x * lax.rsqrt(ms + RMS_EPS) * g


def _dot(a, b):
    return jnp.dot(a, b, preferred_element_type=F32)


def _dot_nt(a, b):
    return lax.dot_general(a, b, (((1,), (1,)), ((), ())), preferred_element_type=F32)


def _split_dot(x, w):
    hi = x.astype(BF16)
    lo = (x - hi.astype(F32)).astype(BF16)
    return _dot(hi, w) + _dot(lo, w)


def _gelu(x):
    return 0.5 * x * (1.0 + jnp.tanh(0.7978845608028654 * (x + 0.044715 * (x * x * x))))


def _head_q(q, h, lane):
    chunk = q[:, (h // 2) * LANES_V7X:(h // 2 + 1) * LANES_V7X]
    keep = (lane < HEAD_DIM) if h % 2 == 0 else (lane >= HEAD_DIM)
    return jnp.where(keep, chunk, jnp.zeros_like(chunk))


def _normproj_kernel(x_ref, g_ref, w_ref, o_ref, hn_ref):
    @pl.when(pl.program_id(1) == 0)
    def _():
        hn_ref[...] = _rms(x_ref[...], g_ref[...]).astype(BF16)

    o_ref[...] = _dot(hn_ref[...], w_ref[...]).astype(o_ref.dtype)


def _normproj(x, g, w, tm=1024, tn=1536):
    s, d = x.shape
    n = w.shape[1]
    return pl.pallas_call(
        _normproj_kernel,
        grid=(s // tm, n // tn),
        in_specs=[pl.BlockSpec((tm, d), lambda i, j: (i, 0)),
                  pl.BlockSpec((1, d), lambda i, j: (0, 0)),
                  pl.BlockSpec((d, tn), lambda i, j: (0, j))],
        out_specs=pl.BlockSpec((tm, tn), lambda i, j: (i, j)),
        out_shape=jax.ShapeDtypeStruct((s, n), BF16),
        scratch_shapes=[pltpu.VMEM((tm, d), BF16)],
        compiler_params=_params(("parallel", "arbitrary"), 48),
        name="normproj",
    )(x, g, w)


def _banded_kernel(*refs, nprev, use_sink):
    q_ref = refs[0]
    k_refs = refs[1:2 + nprev]
    v_refs = refs[2 + nprev:3 + 2 * nprev]
    pos = 3 + 2 * nprev
    if use_sink:
        sink_ref = refs[pos]
        pos += 1
    o_ref = refs[pos]

    n = pl.program_id(0)
    window = nprev * BLK
    row = lax.broadcasted_iota(jnp.int32, (BLK, BLK), 0)
    col = lax.broadcasted_iota(jnp.int32, (BLK, BLK), 1)
    q = q_ref[...]
    outs = []
    for kv in range(N_KV):
        heads = range(kv * GROUP, (kv + 1) * GROUP)
        lanes = slice(kv * LANES_V7X, (kv + 1) * LANES_V7X)
        q4 = jnp.concatenate([_head_q(q, h, col) for h in heads], axis=0)
        logits = []
        for p in range(nprev + 1):
            dist = row - col + (nprev - p) * BLK
            s_pos = (n - nprev + p) * BLK + col
            valid = (s_pos >= 0) & (dist >= 0) & (dist < window)
            distf = dist.astype(F32)
            bias = jnp.concatenate([jnp.where(valid, _slope(h) * distf, -NEG) for h in heads], axis=0)
            logits.append(jnp.maximum(_dot_nt(q4, k_refs[p][:, lanes]) - bias, NEG))
        m = logits[0].max(axis=-1, keepdims=True)
        for lg in logits[1:]:
            m = jnp.maximum(m, lg.max(axis=-1, keepdims=True))
        if use_sink:
            sink = jnp.concatenate(
                [jnp.broadcast_to(sink_ref[:, h:h + 1], (BLK, 1)) for h in heads], axis=0)
            m = jnp.maximum(m, sink)
            den = jnp.exp(sink - m)
        else:
            den = jnp.zeros_like(m)
        acc = jnp.zeros((GROUP * BLK, LANES_V7X), F32)
        for p in range(nprev + 1):
            e = jnp.exp(logits[p] - m)
            den = den + e.sum(axis=-1, keepdims=True)
            acc = acc + _dot(e.astype(BF16), v_refs[p][:, lanes])
        out = acc / den
        outs += [out[g * BLK:(g + 1) * BLK, :] for g in range(GROUP)]
    for j in range(N_HEADS // 2):
        o_ref[:, j * LANES_V7X:(j + 1) * LANES_V7X] = jnp.where(
            col < HEAD_DIM, outs[2 * j], outs[2 * j + 1]).astype(o_ref.dtype)


def _banded(p_all, q_off, k_off, v_off, window, sinks=None):
    s = p_all.shape[0]
    nprev = window // BLK
    kw = 2 * LANES_V7X

    def kv_spec(off, p):
        return pl.BlockSpec((BLK, kw), lambda n: (jnp.maximum(n - nprev + p, 0), off // kw))

    in_specs = [pl.BlockSpec((BLK, MIX_W), lambda n: (n, q_off // MIX_W))]
    in_specs += [kv_spec(k_off, p) for p in range(nprev + 1)]
    in_specs += [kv_spec(v_off, p) for p in range(nprev + 1)]
    args = [p_all] * (3 + 2 * nprev)
    if sinks is not None:
        in_specs.append(pl.BlockSpec((1, N_HEADS), lambda n: (0, 0)))
        args.append(sinks.reshape(1, N_HEADS))
    return pl.pallas_call(
        functools.partial(_banded_kernel, nprev=nprev, use_sink=sinks is not None),
        grid=(s // BLK,),
        in_specs=in_specs,
        out_specs=pl.BlockSpec((BLK, MIX_W), lambda n: (n, 0)),
        out_shape=jax.ShapeDtypeStruct((s, MIX_W), BF16),
        compiler_params=_params(("parallel",), 32),
        name="banded_w%d" % window,
    )(*args)


def _stick_kernel(q_ref, k_ref, v_ref, o_ref, q8_ref, acc_ref, r_ref):
    n = pl.program_id(0)
    rows = N_HEADS * BLK
    pair = 2 * BLK
    row = lax.broadcasted_iota(jnp.int32, (BLK, BLK), 0)
    col = lax.broadcasted_iota(jnp.int32, (BLK, BLK), 1)
    suffix_ones = jnp.concatenate(
        [(row > col).astype(BF16), jnp.ones((BLK, BLK), BF16)], axis=1)
    q = q_ref[...]
    for h in range(N_HEADS):
        q8_ref[h * BLK:(h + 1) * BLK, :] = _head_q(q, h, col)
    acc_ref[...] = jnp.zeros_like(acc_ref)
    r_ref[...] = jnp.zeros_like(r_ref)

    def tile(kb, diag):
        k0 = pl.multiple_of(kb * BLK, BLK)
        z = jnp.concatenate(
            [_dot_nt(q8_ref[j * pair:(j + 1) * pair, :],
                     k_ref[pl.ds(k0, BLK), j * LANES_V7X:(j + 1) * LANES_V7X])
             for j in range(N_HEADS // 2)], axis=0)
        sp = jnp.maximum(z, 0.0) + jnp.log(1.0 + jnp.exp(-jnp.abs(z)))
        if diag:
            t_in = lax.broadcasted_iota(jnp.int32, (rows, BLK), 0) & (BLK - 1)
            valid = lax.broadcasted_iota(jnp.int32, (rows, BLK), 1) < t_in
            spm = jnp.where(valid, sp, 0.0)
        else:
            spm = sp
        ct = _split_dot(spm, suffix_ones)
        a = jnp.exp(z - sp - ct[:, :BLK] - acc_ref[...])
        if diag:
            a = jnp.where(valid, a, 0.0)
        a = a.astype(BF16)
        for j in range(N_HEADS // 2):
            r_ref[j * pair:(j + 1) * pair, :] += _dot(
                a[j * pair:(j + 1) * pair, :],
                v_ref[pl.ds(k0, BLK), j * LANES_V7X:(j + 1) * LANES_V7X])
        acc_ref[...] += ct[:, BLK:]
        return jnp.min(acc_ref[...])

    def cond(carry):
        i, acc_min = carry
        return (i <= n) & (acc_min < STICK_DONE)

    def body(carry):
        i, _ = carry
        return i + 1, tile(n - i, False)

    lax.while_loop(cond, body, (jnp.int32(1), tile(n, True)))
    for j in range(N_HEADS // 2):
        o_ref[:, j * LANES_V7X:(j + 1) * LANES_V7X] = jnp.where(
            col < HEAD_DIM, r_ref[2 * j * BLK:(2 * j + 1) * BLK, :],
            r_ref[(2 * j + 1) * BLK:(2 * j + 2) * BLK, :]).astype(o_ref.dtype)


def _stick(p_all):
    s = p_all.shape[0]
    return pl.pallas_call(
        _stick_kernel,
        grid=(s // BLK,),
        in_specs=[pl.BlockSpec((BLK, MIX_W), lambda n: (n, P_QC // MIX_W)),
                  _resident((s, MIX_W), lambda n: (0, P_KC // MIX_W)),
                  _resident((s, MIX_W), lambda n: (0, P_VC // MIX_W))],
        out_specs=pl.BlockSpec((BLK, MIX_W), lambda n: (n, 0)),
        out_shape=jax.ShapeDtypeStruct((s, MIX_W), BF16),
        scratch_shapes=[pltpu.VMEM((N_HEADS * BLK, LANES_V7X), BF16),
                        pltpu.VMEM((N_HEADS * BLK, BLK), F32),
                        pltpu.VMEM((N_HEADS * BLK, LANES_V7X), F32)],
        compiler_params=_params(("parallel",), 48),
        name="stick_breaking",
    )(p_all, p_all, p_all)


def _compress_kernel(y_ref, w1_ref, pos_ref, w1f_ref, w2_ref, o_ref):
    nc = y_ref.shape[0]
    ab = _dot(y_ref[...], w1_ref[...])
    bias = _dot(pos_ref[...], w1f_ref[...])[0:1, :]
    for hh in range(N_KV):
        top = ab[:, hh * CMP_HID:(hh + 1) * CMP_HID]
        bot = ab[:, (N_KV + hh) * CMP_HID:(N_KV + hh + 1) * CMP_HID]
        hid = top + pltpu.roll(bot, nc - 1, 0) + bias
        o_ref[:, hh * LANES_V7X:(hh + 1) * LANES_V7X] = _dot(
            _gelu(hid).astype(BF16), w2_ref[...]).astype(o_ref.dtype)


def _compress(y, w1, pos, w2):
    nc = y.shape[0]
    half = L_CMP // 2
    zeros = jnp.zeros((half, HEAD_DIM, CMP_HID), F32)

    def place(w, hh):
        parts = [w, zeros] if hh == 0 else [zeros, w]
        return jnp.concatenate(parts, axis=1).reshape(half * 2 * HEAD_DIM, CMP_HID)

    w1cat = jnp.concatenate([place(w1[:half], 0), place(w1[:half], 1),
                             place(w1[half:], 0), place(w1[half:], 1)], axis=1).astype(BF16)
    posf = jnp.broadcast_to(pos.reshape(1, L_CMP * HEAD_DIM), (8, L_CMP * HEAD_DIM)).astype(BF16)
    w1f = w1.reshape(L_CMP * HEAD_DIM, CMP_HID).astype(BF16)
    w2d = jnp.concatenate([w2, w2], axis=1).astype(BF16)
    return pl.pallas_call(
        _compress_kernel,
        out_shape=jax.ShapeDtypeStruct((nc, 2 * LANES_V7X), BF16),
        compiler_params=pltpu.CompilerParams(vmem_limit_bytes=48 * 1024 * 1024),
        name="nsa_compress",
    )(y, w1cat, posf, w1f, w2d)


def _cmpsel_kernel(q_ref, kc_ref, vc_ref, cover_ref, o_ref, sel_ref, any_ref):
    n = pl.program_id(0)
    nc = kc_ref.shape[0]
    nsel = cover_ref.shape[1]
    col = lax.broadcasted_iota(jnp.int32, (BLK, LANES_V7X), 1)
    t = n * BLK + lax.broadcasted_iota(jnp.int32, (BLK, nc), 0)
    c_end = lax.broadcasted_iota(jnp.int32, (BLK, nc), 1) * D_CMP + (L_CMP - 1)
    cvalid = c_end <= t
    cdist = (t - c_end).astype(F32)
    q = q_ref[...]
    outs = []
    for kv in range(N_KV):
        kc = kc_ref[:, kv * LANES_V7X:(kv + 1) * LANES_V7X]
        vc = vc_ref[:, kv * LANES_V7X:(kv + 1) * LANES_V7X]
        psum = jnp.zeros((BLK, nc), F32)
        for g in range(GROUP):
            h = kv * GROUP + g
            z = _dot_nt(_head_q(q, h, col), kc) - _slope(h) * cdist
            z = jnp.where(cvalid, z, NEG)
            e = jnp.exp(z - z.max(axis=-1, keepdims=True))
            p = jnp.where(cvalid, e / e.sum(axis=-1, keepdims=True), 0.0)
            outs.append(_dot(p.astype(BF16), vc))
            psum = psum + p
        imp = _split_dot(psum, cover_ref[...])
        jj = lax.broadcasted_iota(jnp.int32, (BLK, nsel), 1)
        tb = (n * BLK + lax.broadcasted_iota(jnp.int32, (BLK, nsel), 0)) // L_SEL
        forced = (jj == 0) | (jj == tb) | (jj == tb - 1)
        cur = jnp.where(forced, BIG, jnp.where(jj > tb, -BIG, imp))
        picked = jnp.zeros((BLK, nsel), jnp.bool_)
        for _ in range(min(N_SEL, nsel)):
            m = cur.max(axis=-1, keepdims=True)
            first = jnp.where(cur == m, jj, nsel).min(axis=-1, keepdims=True)
            hit = jj == first
            picked = picked | hit
            cur = jnp.where(hit, -BIG, cur)
        chosen = jnp.where(picked, 1.0, 0.0)
        sel_ref[:, kv * nsel:(kv + 1) * nsel] = chosen.astype(sel_ref.dtype)
        any_ref[0, :, kv * nsel:(kv + 1) * nsel] = chosen.max(axis=0, keepdims=True).astype(jnp.int32)
    for j in range(N_HEADS // 2):
        o_ref[:, j * LANES_V7X:(j + 1) * LANES_V7X] = jnp.where(
            col < HEAD_DIM, outs[2 * j], outs[2 * j + 1]).astype(o_ref.dtype)


def _cmpsel(p_all, kcmp, vcmp, cover):
    s = p_all.shape[0]
    nc, nsel = cover.shape
    return pl.pallas_call(
        _cmpsel_kernel,
        grid=(s // BLK,),
        in_specs=[pl.BlockSpec((BLK, MIX_W), lambda n: (n, P_QB // MIX_W)),
                  pl.BlockSpec((nc, 2 * LANES_V7X), lambda n: (0, 0)),
                  pl.BlockSpec((nc, 2 * LANES_V7X), lambda n: (0, 0)),
                  pl.BlockSpec((nc, nsel), lambda n: (0, 0))],
        out_specs=[pl.BlockSpec((BLK, MIX_W), lambda n: (n, 0)),
                   pl.BlockSpec((BLK, N_KV * nsel), lambda n: (n, 0)),
                   pl.BlockSpec((1, 1, N_KV * nsel), lambda n: (n, 0, 0))],
        out_shape=[jax.ShapeDtypeStruct((s, MIX_W), BF16),
                   jax.ShapeDtypeStruct((s, N_KV * nsel), BF16),
                   jax.ShapeDtypeStruct((s // BLK, 1, N_KV * nsel), jnp.int32)],
        compiler_params=_params(("parallel",), 48),
        name="nsa_cmp_select",
    )(p_all, kcmp, vcmp, cover)


def _selattn_kernel(q_ref, sel_ref, any_ref, k_ref, v_ref, o_ref, q8_ref, m_ref, l_ref, r_ref):
    n = pl.program_id(0)
    nsel = sel_ref.shape[1] // N_KV
    grp = GROUP * BLK
    row = lax.broadcasted_iota(jnp.int32, (BLK, BLK), 0)
    col = lax.broadcasted_iota(jnp.int32, (BLK, BLK), 1)
    q = q_ref[...]
    for h in range(N_HEADS):
        q8_ref[h * BLK:(h + 1) * BLK, :] = _head_q(q, h, col)
    m_ref[...] = jnp.full_like(m_ref, NEG)
    l_ref[...] = jnp.zeros_like(l_ref)
    r_ref[...] = jnp.zeros_like(r_ref)
    ej = lax.broadcasted_iota(jnp.int32, (nsel, BLK), 0)
    ek = lax.broadcasted_iota(jnp.int32, (nsel, BLK), 1) // L_SEL

    def body(c, carry):
        per_chunk = BLK // L_SEL
        hits = any_ref[0, 0, per_chunk * c]
        for kv in range(N_KV):
            for b in range(per_chunk):
                if kv + b > 0:
                    hits = hits + any_ref[0, 0, kv * nsel + per_chunk * c + b]

        @pl.when(hits > 0)
        def _():
            k0 = pl.multiple_of(c * BLK, BLK)
            dist = (n - c) * BLK + row - col
            distf = dist.astype(F32)
            expand = jnp.where(ej == c * per_chunk + ek, 1.0, 0.0).astype(BF16)
            valids, zs = [], []
            for kv in range(N_KV):
                chosen = _dot(sel_ref[:, kv * nsel:(kv + 1) * nsel], expand)
                valids += [(chosen > 0.5) & (dist >= 0)] * GROUP
                zs.append(_dot_nt(q8_ref[kv * grp:(kv + 1) * grp, :],
                                  k_ref[pl.ds(k0, BLK), kv * LANES_V7X:(kv + 1) * LANES_V7X]))
            valid = jnp.concatenate(valids, axis=0)
            bias = jnp.concatenate([_slope(h) * distf for h in range(N_HEADS)], axis=0)
            z = jnp.where(valid, jnp.concatenate(zs, axis=0) - bias, NEG)
            m_old = m_ref[...]
            m_new = jnp.maximum(m_old, z.max(axis=-1, keepdims=True))
            alpha = jnp.exp(m_old - m_new)
            e = jnp.where(valid, jnp.exp(z - m_new), 0.0)
            l_ref[...] = alpha * l_ref[...] + e.sum(axis=-1, keepdims=True)
            eb = e.astype(BF16)
            pv = jnp.concatenate(
                [_dot(eb[kv * grp:(kv + 1) * grp, :],
                      v_ref[pl.ds(k0, BLK), kv * LANES_V7X:(kv + 1) * LANES_V7X])
                 for kv in range(N_KV)], axis=0)
            r_ref[...] = alpha * r_ref[...] + pv
            m_ref[...] = m_new
        return carry

    lax.fori_loop(0, n + 1, body, 0)
    out = r_ref[...] / l_ref[...]
    for j in range(N_HEADS // 2):
        o_ref[:, j * LANES_V7X:(j + 1) * LANES_V7X] = jnp.where(
            col < HEAD_DIM, out[2 * j * BLK:(2 * j + 1) * BLK, :],
            out[(2 * j + 1) * BLK:(2 * j + 2) * BLK, :]).astype(o_ref.dtype)


def _selattn(p_all, sel, sel_any):
    s = p_all.shape[0]
    kw = 2 * LANES_V7X
    return pl.pallas_call(
        _selattn_kernel,
        grid=(s // BLK,),
        in_specs=[pl.BlockSpec((BLK, MIX_W), lambda n: (n, P_QB // MIX_W)),
                  pl.BlockSpec((BLK, sel.shape[1]), lambda n: (n, 0)),
                  pl.BlockSpec((1, 1, sel.shape[1]), lambda n: (n, 0, 0), memory_space=pltpu.SMEM),
                  _resident((s, kw), lambda n: (0, P_KS // kw)),
                  _resident((s, kw), lambda n: (0, P_VS // kw))],
        out_specs=pl.BlockSpec((BLK, MIX_W), lambda n: (n, 0)),
        out_shape=jax.ShapeDtypeStruct((s, MIX_W), BF16),
        scratch_shapes=[pltpu.VMEM((N_HEADS * BLK, LANES_V7X), BF16),
                        pltpu.VMEM((N_HEADS * BLK, 1), F32),
                        pltpu.VMEM((N_HEADS * BLK, 1), F32),
                        pltpu.VMEM((N_HEADS * BLK, LANES_V7X), F32)],
        compiler_params=_params(("parallel",), 48),
        name="nsa_selected",
    )(p_all, sel, sel_any, p_all, p_all)


def _merge_kernel(x_ref, g1_ref, g2_ref, oa_ref, ocmp_ref, oslc_ref, owin_ref, oc_ref, gb_ref,
                  gexp_ref, wgm_ref, wbr_ref, wout_ref, xo_ref, hnt_ref):
    x = x_ref[...]
    hn = _rms(x, g1_ref[...]).astype(BF16)
    gb = jax.nn.sigmoid(gb_ref[...].astype(F32))
    gexp = _split_dot(gb, gexp_ref[...])
    ob = (gexp[:, 0:MIX_W] * ocmp_ref[...].astype(F32)
          + gexp[:, MIX_W:2 * MIX_W] * oslc_ref[...].astype(F32)
          + gexp[:, 2 * MIX_W:3 * MIX_W] * owin_ref[...].astype(F32)).astype(BF16)
    mixed = jnp.zeros(x.shape, F32)
    for i, o in enumerate((oa_ref[...], ob, oc_ref[...])):
        gm = jax.nn.sigmoid(_dot(hn, wgm_ref[:, i * D_MODEL:(i + 1) * D_MODEL]))
        mixed = mixed + gm * _dot(o, wbr_ref[i])
    xn = x + _dot(mixed.astype(BF16), wout_ref[...])
    xo_ref[...] = xn
    hnt_ref[...] = _rms(xn, g2_ref[...]).T.astype(BF16)


def _merge(x, g1, g2, oa, ocmp, oslc, owin, oc, p_all, gexp, wgm, wbr, wout, tm=256):
    s, d = x.shape
    row = lambda w: pl.BlockSpec((tm, w), lambda i: (i, 0))
    return pl.pallas_call(
        _merge_kernel,
        grid=(s // tm,),
        in_specs=[row(d),
                  pl.BlockSpec((1, d), lambda i: (0, 0)),
                  pl.BlockSpec((1, d), lambda i: (0, 0)),
                  row(MIX_W), row(MIX_W), row(MIX_W), row(MIX_W), row(MIX_W),
                  pl.BlockSpec((tm, LANES_V7X), lambda i: (i, P_GB // LANES_V7X)),
                  _resident(gexp.shape, lambda i: (0, 0)),
                  _resident(wgm.shape, lambda i: (0, 0)),
                  _resident(wbr.shape, lambda i: (0, 0, 0)),
                  _resident(wout.shape, lambda i: (0, 0))],
        out_specs=[row(d), pl.BlockSpec((d, tm), lambda i: (0, i))],
        out_shape=[jax.ShapeDtypeStruct((s, d), F32), jax.ShapeDtypeStruct((d, s), BF16)],
        compiler_params=_params(("parallel",), 48),
        name="merge",
    )(x, g1, g2, oa, ocmp, oslc, owin, oc, p_all, gexp, wgm, wbr, wout)


SUBLANES_V7X = 8


def _exchange(seq, i, l):
    hi, lo = jnp.maximum(seq[i], seq[l]), jnp.minimum(seq[i], seq[l])
    seq[i], seq[l] = hi, lo


def _bitonic_merge(seq):
    n = len(seq)
    j = n // 2
    while j >= 1:
        for i in range(n):
            if i ^ j > i:
                _exchange(seq, i, i ^ j)
        j //= 2


def _top_values(tiles):
    seq = list(tiles)
    n = len(seq)
    k = 2
    while k <= n:
        j = k // 2
        while j >= 1:
            for i in range(n):
                l = i ^ j
                if l > i:
                    if i & k == 0:
                        _exchange(seq, i, l)
                    else:
                        _exchange(seq, l, i)
            j //= 2
        k *= 2
    lost = jnp.full(seq[0].shape, -BIG, F32)
    shift = SUBLANES_V7X // 2
    while shift >= 1:
        other = [pltpu.roll(s, shift, 0) for s in seq]
        lost = jnp.maximum(lost, pltpu.roll(lost, shift, 0))
        merged = []
        for r in range(n):
            merged.append(jnp.maximum(seq[r], other[n - 1 - r]))
            lost = jnp.maximum(lost, jnp.minimum(seq[r], other[n - 1 - r]))
        _bitonic_merge(merged)
        seq = merged
        shift //= 2
    return seq + [lost]


def _route_kernel(xnt_ref, wq_ref, keys_ref, thr_ref, e1_ref, e2_ref, qr_ref):
    qr_ref[...] = _dot(wq_ref[...], xnt_ref[...]).astype(BF16)
    tm = qr_ref.shape[1]
    sub = lax.broadcasted_iota(jnp.int32, (SUBLANES_V7X, tm), 0)
    fill = jnp.full((SUBLANES_V7X, tm), -BIG, F32)

    def tiles_of(x):
        return [x[r * SUBLANES_V7X:(r + 1) * SUBLANES_V7X, :] for r in range(x.shape[0] // SUBLANES_V7X)]

    def spread(vals):
        out = vals[0]
        for r in range(1, len(vals)):
            out = jnp.where(sub == r, vals[r], out)
        return out

    for h in range(PEER_HEADS):
        base = h * 2 * PK_DIM
        s1 = _dot(keys_ref[0, h], qr_ref[base:base + PK_DIM, :])
        s2 = _dot(keys_ref[1, h], qr_ref[base + PK_DIM:base + 2 * PK_DIM, :])
        v1 = _top_values(tiles_of(s1))
        v2 = _top_values(tiles_of(s2))
        v2_lo, v2_hi, v1_hi = spread(v2[0:8]), spread(v2[8:16]), spread(v1[8:16])
        cand = [v1[0] + v2_lo, v1[0] + v2_hi]
        cand += [v1[a] + v2_lo for a in range(1, 8)]
        cand += [v1_hi + v2[0], spread([v1[0] + v2[16], v1[16] + v2[0]] + [fill] * 6)]
        cand += [fill] * (PEER_TOPK - len(cand))
        sc = _top_values(cand)
        z = sc[0] - sc[0]
        for r in range(PEER_TOPK):
            z = z + jnp.exp(sc[r] - sc[0])
        tau = 0.5 * (sc[PEER_TOPK - 1] + sc[PEER_TOPK])[0:1, :]
        m1 = v1[0][0:1, :]
        m2 = v2[0][0:1, :]
        thr_ref[h] = jnp.exp(tau - m2 - s1)
        e1_ref[h] = jnp.exp(s1 - m1) / z[0:1, :]
        e2_ref[h] = jnp.exp(s2 - m2)


def _route(xnt, wq_t, keys, tm=256):
    d, s = xnt.shape
    big = lambda: pl.BlockSpec((PEER_HEADS, N_KEYS, tm), lambda i: (0, 0, i))
    big_shape = jax.ShapeDtypeStruct((PEER_HEADS, N_KEYS, s), F32)
    return pl.pallas_call(
        _route_kernel,
        grid=(s // tm,),
        in_specs=[pl.BlockSpec((d, tm), lambda i: (0, i)),
                  _resident(wq_t.shape, lambda i: (0, 0)),
                  _resident(keys.shape, lambda i: (0, 0, 0, 0))],
        out_specs=[big(), big(), big()],
        out_shape=[big_shape, big_shape, big_shape],
        scratch_shapes=[pltpu.VMEM((PEER_HEADS * 2 * PK_DIM, tm), BF16)],
        compiler_params=_params(("parallel",), 48),
        name="peer_route",
    )(xnt, wq_t, keys)


def _experts_kernel(x_ref, xnt_ref, u_ref, vt_ref, thr_ref, e1_ref, e2_ref,
                    o_ref, acc_ref, g_ref, w_ref, *, chunks):
    j = pl.program_id(1)

    @pl.when(j == 0)
    def _():
        acc_ref[...] = jnp.zeros_like(acc_ref)

    xnt = xnt_ref[...]
    for c in range(chunks):
        rows = slice(c * N_KEYS, (c + 1) * N_KEYS)
        g_ref[rows, :] = _dot(u_ref[rows, :], xnt)
        i1 = j * chunks + c
        w = None
        for h in range(PEER_HEADS):
            e2 = e2_ref[h]
            t = e1_ref[h, pl.ds(i1, 1), :] * jnp.where(e2 >= thr_ref[h, pl.ds(i1, 1), :], e2, 0.0)
            w = t if w is None else w + t
        w_ref[rows, :] = w
    acc_ref[...] += _dot(vt_ref[...], (_gelu(g_ref[...]) * w_ref[...]).astype(BF16))

    @pl.when(j == pl.num_programs(1) - 1)
    def _():
        o_ref[...] = x_ref[...] + acc_ref[...].T


def _experts(x, xnt, u, vt, thr, e1, e2, tm=512, te=2048):
    s, d = x.shape
    big = lambda: pl.BlockSpec((PEER_HEADS, N_KEYS, tm), lambda i, j: (0, 0, i))
    return pl.pallas_call(
        functools.partial(_experts_kernel, chunks=te // N_KEYS),
        grid=(s // tm, N_EXPERTS // te),
        in_specs=[pl.BlockSpec((tm, d), lambda i, j: (i, 0)),
                  pl.BlockSpec((d, tm), lambda i, j: (0, i)),
                  pl.BlockSpec((te, d), lambda i, j: (j, 0)),
                  pl.BlockSpec((d, te), lambda i, j: (0, j)),
                  big(), big(), big()],
        out_specs=pl.BlockSpec((tm, d), lambda i, j: (i, 0)),
        out_shape=jax.ShapeDtypeStruct((s, d), F32),
        scratch_shapes=[pltpu.VMEM((d, tm), F32), pltpu.VMEM((te, tm), F32), pltpu.VMEM((te, tm), F32)],
        compiler_params=_params(("parallel", "arbitrary"), 56),
        name="peer_experts",
    )(x, xnt, u, vt, thr, e1, e2)


def _norm_kernel(x_ref, g_ref, o_ref):
    o_ref[...] = _rms(x_ref[...], g_ref[...])


def _final_norm(x, g, tm=1024):
    s, d = x.shape
    return pl.pallas_call(
        _norm_kernel,
        grid=(s // tm,),
        in_specs=[pl.BlockSpec((tm, d), lambda i: (i, 0)), pl.BlockSpec((1, d), lambda i: (0, 0))],
        out_specs=pl.BlockSpec((tm, d), lambda i: (i, 0)),
        out_shape=jax.ShapeDtypeStruct((s, d), F32),
        compiler_params=_params(("parallel",), 32),
        name="final_norm",
    )(x, g)


def _dup_heads(w):
    a, b = w[:, :HEAD_DIM], w[:, HEAD_DIM:]
    return jnp.concatenate([a, a, b, b], axis=1)


def _pack_w_in(w_in):
    parts, start = [], 0
    for w in IN_WIDTHS:
        parts.append(w_in[:, start:start + w])
        start += w
    qa, ka, va, qb, kcb, vcb, ksb, vsb, kwb, vwb, gb, qc, kc, vc, gm = parts
    scale = HEAD_DIM ** -0.5
    d = w_in.shape[0]
    packed = jnp.concatenate(
        [qc * scale, kc, vc, qa * scale, qb * scale,
         _dup_heads(ka), _dup_heads(va), _dup_heads(ksb), _dup_heads(vsb),
         _dup_heads(kwb), _dup_heads(vwb), kcb, vcb,
         gb, jnp.zeros((d, P_COLS - P_GB - gb.shape[1]), F32)], axis=1)
    return packed.astype(BF16), gm.astype(BF16)


def _gate_expand():
    r = jnp.arange(LANES_V7X)[:, None]
    c = jnp.arange(3 * MIX_W)[None, :]
    return ((r == 3 * ((c % MIX_W) // HEAD_DIM) + c // MIX_W) & (r < 3 * N_HEADS)).astype(BF16)


def _cover(nc, nsel):
    c_start = jnp.arange(nc)[:, None] * D_CMP
    j_start = jnp.arange(nsel)[None, :] * L_SEL
    real = jnp.arange(nc)[:, None] < nc - 1
    return (real & (c_start < j_start + L_SEL) & (c_start + L_CMP - 1 >= j_start)).astype(BF16)


def _layer(x, norm1, w_in, sinks, cmp_pos, cmp_w1, cmp_w2, w_branch, w_out,
           norm2, peer_wq, peer_keys, peer_u, peer_v):
    s = x.shape[0]
    nc = s // D_CMP
    w_pack, w_gm = _pack_w_in(w_in)
    p_all = _normproj(x, norm1.reshape(1, -1), w_pack)

    o_a = _banded(p_all, P_QA, P_KA, P_VA, A_WINDOW, sinks)
    o_c = _stick(p_all)

    y_k = p_all[:, P_KCB:P_KCB + LANES_V7X].reshape(nc, D_CMP * LANES_V7X)
    y_v = p_all[:, P_VCB:P_VCB + LANES_V7X].reshape(nc, D_CMP * LANES_V7X)
    k_cmp = _compress(y_k, cmp_w1[0], cmp_pos[0], cmp_w2[0])
    v_cmp = _compress(y_v, cmp_w1[1], cmp_pos[1], cmp_w2[1])
    o_cmp, sel, sel_any = _cmpsel(p_all, k_cmp, v_cmp, _cover(nc, s // L_SEL))
    o_slc = _selattn(p_all, sel, sel_any)
    o_win = _banded(p_all, P_QB, P_KW, P_VW, B_WINDOW)

    x, xnt = _merge(x, norm1.reshape(1, -1), norm2.reshape(1, -1), o_a, o_cmp, o_slc, o_win, o_c,
                   p_all, _gate_expand(), w_gm, w_branch.astype(BF16), w_out.astype(BF16))

    wq_t = peer_wq.reshape(D_MODEL, PEER_HEADS * 2 * PK_DIM).T.astype(BF16)
    thr, e1, e2 = _route(xnt, wq_t, peer_keys.astype(BF16))
    return _experts(x, xnt, peer_u.astype(BF16), peer_v.T.astype(BF16), thr, e1, e2)


def kernel(x, norm1, w_in, sinks, cmp_pos, cmp_w1, cmp_w2, w_branch, w_out, norm2, peer_wq,
           peer_keys, peer_u, peer_v, norm_f):
    b, s, d = x.shape
    assert b == 1 and d == D_MODEL and s % (8 * BLK) == 0
    h = x.reshape(s, d)
    for l in range(norm1.shape[0]):
        h = _layer(h, norm1[l], w_in[l], sinks[l], cmp_pos[l], cmp_w1[l], cmp_w2[l],
                   w_branch[l], w_out[l], norm2[l], peer_wq[l], peer_keys[l],
                   peer_u[l], peer_v[l])
    return _final_norm(h, norm_f.reshape(1, -1)).reshape(b, s, d)
```
